```python
import math
import jax
import jax.numpy as jnp
from jax import lax
import numpy as np

D_MODEL = 1024
BATCH = 4
SEQ = 8192
DEPTH = 1
DEC_BATCH = 32
DEC_SEQ = 8
PAST_LEN = 16384
PAGE_SIZE = 128

N_HEADS_A = 8
HEAD_DIM_A = D_MODEL // N_HEADS_A
WIDTH_A = N_HEADS_A * HEAD_DIM_A
MOBA_BLOCK = 256
MOBA_TOPK = 3
MOBA_QCHUNK = 8
ROPE_THETA = 10000.0
N_HEADS_B = 4
DV_B = D_MODEL // N_HEADS_B
DK_B = DV_B // 2
WIDTH_QK_B = N_HEADS_B * DK_B
WIDTH_B = N_HEADS_B * DV_B
MLSTM_CHUNK = 64
FORGET_BIAS = 3.0
D_FF = 2816
CONV_W = 3
RMS_EPS = 1e-6
IN_SPLITS = (WIDTH_A, WIDTH_A, WIDTH_A, WIDTH_QK_B, WIDTH_QK_B, WIDTH_B, WIDTH_B, N_HEADS_B, N_HEADS_B, WIDTH_A, WIDTH_B)
D_IN = sum(IN_SPLITS)

kernel_name = "hybrid_moba_mlstm_convffn_step"


def _rms_norm(x, g):
    xf = x.astype(jnp.float32)
    y = xf * lax.rsqrt(jnp.mean(xf * xf, axis=-1, keepdims=True) + RMS_EPS)
    return (y * g.astype(jnp.float32)).astype(x.dtype)


def _modulate(h, shift, scale):
    return h * (1 + scale[:, None, :]) + shift[:, None, :]


def _rope(x, pos):
    half = x.shape[-1] // 2
    inv = ROPE_THETA ** (-jnp.arange(half, dtype=jnp.float32) / half)
    ang = pos.astype(jnp.float32)[:, None] * inv[None, :]
    cos = jnp.cos(ang)[None, :, None, :]
    sin = jnp.sin(ang)[None, :, None, :]
    xf = x.astype(jnp.float32)
    x1, x2 = xf[..., :half], xf[..., half:]
    return jnp.concatenate([x1 * cos - x2 * sin, x2 * cos + x1 * sin], axis=-1).astype(x.dtype)


def _moba_attention(q, k_all, v_all, q_pos):
    B, T, H, hd = q.shape
    L = k_all.shape[1]
    n_full = L // MOBA_BLOCK
    n_sel = min(MOBA_TOPK, n_full)
    own_blk = q_pos // MOBA_BLOCK
    if n_sel > 0:
        kmean = jnp.mean(k_all[:, :n_full * MOBA_BLOCK].astype(jnp.float32).reshape(B, n_full, MOBA_BLOCK, H, hd), axis=2)
        gate = jnp.einsum('bthd,bnhd->bhtn', q.astype(jnp.float32), kmean)
        fully_past = jnp.arange(n_full)[None, :] < own_blk[:, None]
        gate = jnp.where(fully_past[None, None], gate, -jnp.inf)
        _, sel = lax.top_k(gate, n_sel)
        sel = sel.astype(jnp.int32)
    else:
        sel = jnp.zeros((B, H, T, 0), jnp.int32)
    qc = math.gcd(T, MOBA_QCHUNK)
    nch = T // qc
    q_ch = jnp.moveaxis(q.reshape(B, nch, qc, H, hd), 1, 0)
    pos_ch = q_pos.reshape(nch, qc)
    sel_ch = jnp.moveaxis(sel.reshape(B, H, nch, qc, n_sel), 2, 0)
    offs = jnp.arange(MOBA_BLOCK, dtype=jnp.int32)
    bi = jnp.arange(B)[:, None, None, None]
    hi = jnp.arange(H)[None, :, None, None]
    scale = hd ** -0.5

    def one_chunk(args):
        q_c, pos_c, sel_c = args
        own = (pos_c // MOBA_BLOCK)[None, None, :, None]
        blk = jnp.concatenate([sel_c, jnp.broadcast_to(own, (B, H, qc, 1))], axis=-1)
        kpos5 = blk[..., None] * MOBA_BLOCK + offs
        sel_ok = jnp.concatenate([sel_c < own, jnp.ones((B, H, qc, 1), bool)], axis=-1)
        mask = (sel_ok[..., None] & (kpos5 <= pos_c[None, None, :, None, None])).reshape(B, H, qc, -1)
        idx = jnp.minimum(kpos5.reshape(B, H, qc, -1), L - 1)
        kg = k_all[bi, idx, hi]
        vg = v_all[bi, idx, hi]
        s = jnp.einsum('bqhd,bhqkd->bhqk', q_c, kg).astype(jnp.float32) * scale
        p = jax.nn.softmax(jnp.where(mask, s, -jnp.inf), axis=-1).astype(v_all.dtype)
        return jnp.einsum('bhqk,bhqkd->bqhd', p, vg)

    out = lax.map(one_chunk, (q_ch, pos_ch, sel_ch))
    return jnp.moveaxis(out, 0, 1).reshape(B, T, H * hd)


def _mlstm(q, k, v, i_pre, f_pre, C0, n0, m0):
    B, H, T, dk = q.shape
    dv = v.shape[-1]
    Lc = math.gcd(T, MLSTM_CHUNK)
    nc = T // Lc
    f32 = jnp.float32

    def to_chunks(a):
        return jnp.moveaxis(a.astype(f32).reshape(B, H, nc, Lc, *a.shape[3:]), 2, 0)

    causal = jnp.tril(jnp.ones((Lc, Lc), bool))

    def step(carry, xs):
        C, n, m = carry
        qc_, kc, vc, ic, lfc = xs
        b = jnp.cumsum(lfc, axis=-1)
        dmat = jnp.where(causal, b[..., :, None] - b[..., None, :] + ic[..., None, :], -jnp.inf)
        inter = b + m[..., None]
        m_t = jnp.maximum(inter, jnp.max(dmat, axis=-1))
        w = jnp.exp(dmat - m_t[..., None])
        a_inter = jnp.exp(inter - m_t)
        qk = jnp.einsum('bhtd,bhsd->bhts', qc_, kc) * w
        num = a_inter[..., None] * jnp.einsum('bhvd,bhtd->bhtv', C, qc_) + jnp.einsum('bhts,bhsv->bhtv', qk, vc)
        den = a_inter * jnp.einsum('bhd,bhtd->bht', n, qc_) + jnp.sum(qk, axis=-1)
        h = num / jnp.maximum(jnp.abs(den), jnp.exp(-m_t))[..., None]
        b_L = b[..., -1]
        g = b_L[..., None] - b + ic
        m_new = jnp.maximum(b_L + m, jnp.max(g, axis=-1))
        decay = jnp.exp(b_L + m - m_new)
        ws = jnp.exp(g - m_new[..., None])
        C_new = decay[..., None, None] * C + jnp.einsum('bhs,bhsv,bhsd->bhvd', ws, vc, kc)
        n_new = decay[..., None] * n + jnp.einsum('bhs,bhsd->bhd', ws, kc)
        return (C_new, n_new, m_new), h

    logf = jax.nn.log_sigmoid(f_pre.astype(f32))
    (C, n, m), h = lax.scan(step, (C0.astype(f32), n0.astype(f32), m0.astype(f32)),
                            (to_chunks(q), to_chunks(k), to_chunks(v), to_chunks(i_pre), to_chunks(logf)))
    h = jnp.moveaxis(h, 0, 2).reshape(B, H, T, dv)
    return h, C, n, m


def _conv_ffn(h, conv0, w_up, w_conv, b_conv, w_down):
    T = h.shape[1]
    a, gb = jnp.split(h @ w_up, 2, axis=-1)
    a_pad = jnp.concatenate([conv0.astype(a.dtype), a], axis=1)
    a_conv = b_conv
    for w in range(CONV_W):
        a_conv = a_conv + a_pad[:, w:w + T] * w_conv[w]
    out = (jax.nn.gelu(a_conv) * gb) @ w_down
    return out, a_pad[:, -(CONV_W - 1):]


def _layer(x, c, q_pos, paged, C0, n0, m0, conv0,
           w_ada, b_ada, g_mix, w_in, b_gates, w_out, g_ffn, w_up, w_conv, b_conv, w_down):
    B, T, _ = x.shape
    mod = jax.nn.silu(c) @ w_ada + b_ada
    sh_m, sc_m, gt_m, sh_f, sc_f, gt_f = jnp.split(mod, 6, axis=-1)
    h = _modulate(_rms_norm(x, g_mix), sh_m, sc_m)
    bounds = np.cumsum(IN_SPLITS)[:-1].tolist()
    qa, ka, va, qb, kb, vb, ob, ib, fb, ga, gb = jnp.split(h @ w_in, bounds, axis=-1)
    qa = _rope(qa.reshape(B, T, N_HEADS_A, HEAD_DIM_A), q_pos)
    ka = _rope(ka.reshape(B, T, N_HEADS_A, HEAD_DIM_A), q_pos)
    va = va.reshape(B, T, N_HEADS_A, HEAD_DIM_A)
    if paged is None:
        k_all, v_all = ka, va
    else:
        cache_k, cache_v, page_table, d = paged
        k_all = jnp.concatenate([cache_k[d, page_table].reshape(B, -1, N_HEADS_A, HEAD_DIM_A).astype(ka.dtype), ka], axis=1)
        v_all = jnp.concatenate([cache_v[d, page_table].reshape(B, -1, N_HEADS_A, HEAD_DIM_A).astype(va.dtype), va], axis=1)
    y_a = _moba_attention(qa, k_all, v_all, q_pos)
    qb = qb.reshape(B, T, N_HEADS_B, DK_B).transpose(0, 2, 1, 3)
    kb = kb.reshape(B, T, N_HEADS_B, DK_B).transpose(0, 2, 1, 3) * (DK_B ** -0.5)
    vb = vb.reshape(B, T, N_HEADS_B, DV_B).transpose(0, 2, 1, 3)
    i_pre = (ib + b_gates[:N_HEADS_B]).transpose(0, 2, 1)
    f_pre = (fb + b_gates[N_HEADS_B:]).transpose(0, 2, 1)
    hb, C, n, m = _mlstm(qb, kb, vb, i_pre, f_pre, C0, n0, m0)
    y_b = hb.transpose(0, 2, 1, 3).reshape(B, T, WIDTH_B).astype(x.dtype) * jax.nn.sigmoid(ob)
    merged = jax.nn.sigmoid(ga) * y_a + jax.nn.sigmoid(gb) * y_b
    x = x + gt_m[:, None, :] * (merged @ w_out)
    h = _modulate(_rms_norm(x, g_ffn), sh_f, sc_f)
    f, conv_new = _conv_ffn(h, conv0, w_up, w_conv, b_conv, w_down)
    x = x + gt_f[:, None, :] * f
    return x, ka, va, C, n, m, conv_new


def setup_inputs(seed: int = 0) -> dict:
    key = jax.random.key(seed)
    ks = jax.random.split(key, 24)
    f32 = jnp.float32
    n_pages = PAST_LEN // PAGE_SIZE
    n_used = DEC_BATCH * n_pages
    n_pool = n_used + n_used // 4

    def nrm(k, shape, s=1.0):
        return s * jax.random.normal(k, shape, f32)

    page_table = jax.random.permutation(ks[4], n_pool)[:n_used].reshape(DEC_BATCH, n_pages).astype(jnp.int32)
    b_gates = jnp.concatenate([nrm(ks[15], (DEPTH, N_HEADS_B), 0.1),
                               FORGET_BIAS + nrm(ks[16], (DEPTH, N_HEADS_B), 0.1)], axis=-1)
    return {
        "x_prompt": nrm(ks[0], (BATCH, SEQ, D_MODEL)),
        "x_sample": nrm(ks[1], (DEC_BATCH, DEC_SEQ, D_MODEL)),
        "cache_k": nrm(ks[2], (DEPTH, n_pool, PAGE_SIZE, N_HEADS_A, HEAD_DIM_A)),
        "cache_v": nrm(ks[3], (DEPTH, n_pool, PAGE_SIZE, N_HEADS_A, HEAD_DIM_A)),
        "page_table": page_table,
        "state_C": nrm(ks[5], (DEPTH, DEC_BATCH, N_HEADS_B, DV_B, DK_B), 0.1),
        "state_n": nrm(ks[6], (DEPTH, DEC_BATCH, N_HEADS_B, DK_B), 0.1),
        "state_m": nrm(ks[7], (DEPTH, DEC_BATCH, N_HEADS_B), 0.5),
        "state_conv": nrm(ks[8], (DEPTH, DEC_BATCH, CONV_W - 1, D_FF), 0.5),
        "c_prompt": nrm(ks[9], (BATCH, D_MODEL)),
        "c_sample": nrm(ks[10], (DEC_BATCH, D_MODEL)),
        "w_ada": nrm(ks[11], (DEPTH, D_MODEL, 6 * D_MODEL), 0.5 * D_MODEL ** -0.5),
        "b_ada": nrm(ks[12], (DEPTH, 6 * D_MODEL), 0.02),
        "g_norm_mix": 1.0 + nrm(ks[13], (DEPTH, D_MODEL), 0.02),
        "w_in": nrm(ks[14], (DEPTH, D_MODEL, D_IN), D_MODEL ** -0.5),
        "b_gates": b_gates,
        "w_out": nrm(ks[17], (DEPTH, D_MODEL, D_MODEL), D_MODEL ** -0.5),
        "g_norm_ffn": 1.0 + nrm(ks[18], (DEPTH, D_MODEL), 0.02),
        "w_up": nrm(ks[19], (DEPTH, D_MODEL, 2 * D_FF), D_MODEL ** -0.5),
        "w_conv": nrm(ks[20], (DEPTH, CONV_W, D_FF), CONV_W ** -0.5),
        "b_conv": nrm(ks[21], (DEPTH, D_FF), 0.02),
        "w_down": nrm(ks[22], (DEPTH, D_FF, D_MODEL), D_FF ** -0.5),
        "g_norm_final": 1.0 + nrm(ks[23], (D_MODEL,), 0.02),
    }


def reference(x_prompt, x_sample, cache_k, cache_v, page_table, state_C, state_n, state_m, state_conv,
              c_prompt, c_sample, w_ada, b_ada, g_norm_mix, w_in, b_gates, w_out, g_norm_ffn,
              w_up, w_conv, b_conv, w_down, g_norm_final):
    Bp, Tp, _ = x_prompt.shape
    Bs, Ts, _ = x_sample.shape
    past_len = page_table.shape[1] * cache_k.shape[2]
    pos_p = jnp.arange(Tp, dtype=jnp.int32)
    pos_s = past_len + jnp.arange(Ts, dtype=jnp.int32)
    xp, xs = x_prompt, x_sample
    kp_l, vp_l, ks_l, vs_l = [], [], [], []
    Cp_l, np_l, mp_l, Cs_l, ns_l, ms_l, cvp_l, cvs_l = [], [], [], [], [], [], [], []
    for d in range(DEPTH):
        w = (w_ada[d], b_ada[d], g_norm_mix[d], w_in[d], b_gates[d], w_out[d], g_norm_ffn[d],
             w_up[d], w_conv[d], b_conv[d], w_down[d])
        C0 = jnp.zeros((Bp, N_HEADS_B, DV_B, DK_B), jnp.float32)
        n0 = jnp.zeros((Bp, N_HEADS_B, DK_B), jnp.float32)
        m0 = jnp.zeros((Bp, N_HEADS_B), jnp.float32)
        cv0 = jnp.zeros((Bp, CONV_W - 1, D_FF), x_prompt.dtype)
        xp, kp, vp, Cp, n_p, mp, cvp = _layer(xp, c_prompt, pos_p, None, C0, n0, m0, cv0, *w)
        xs, ks_, vs_, Cs, n_s, ms, cvs = _layer(xs, c_sample, pos_s, (cache_k, cache_v, page_table, d),
                                              state_C[d], state_n[d], state_m[d], state_conv[d], *w)
        kp_l.append(kp); vp_l.append(vp); ks_l.append(ks_); vs_l.append(vs_)
        Cp_l.append(Cp.astype(state_C.dtype)); np_l.append(n_p.astype(state_n.dtype)); mp_l.append(mp.astype(state_m.dtype))
        Cs_l.append(Cs.astype(state_C.dtype)); ns_l.append(n_s.astype(state_n.dtype)); ms_l.append(ms.astype(state_m.dtype))
        cvp_l.append(cvp.astype(state_conv.dtype)); cvs_l.append(cvs.astype(state_conv.dtype))
    y_prompt = _rms_norm(xp, g_norm_final)
    y_sample = _rms_norm(xs, g_norm_final)
    return (y_prompt, y_sample,
            jnp.stack(kp_l), jnp.stack(vp_l), jnp.stack(ks_l), jnp.stack(vs_l),
            jnp.stack(Cp_l), jnp.stack(np_l), jnp.stack(mp_l),
            jnp.stack(Cs_l), jnp.stack(ns_l), jnp.stack(ms_l),
            jnp.stack(cvp_l), jnp.stack(cvs_l))
```

```python
import functools
import math

import jax
import jax.numpy as jnp
from jax import lax
from jax.experimental import pallas as pl
from jax.experimental.pallas import tpu as pltpu

F32 = jnp.float32
BF16 = jnp.bfloat16

N_HEADS_A = 8
HEAD_DIM_A = 128
MOBA_BLOCK = 256
MOBA_TOPK = 3
ROPE_THETA = 10000.0
N_HEADS_B = 4
DK_B = 128
DV_B = 256
CONV_W = 3
RMS_EPS = 1e-6
MASKED = -1e30
LANES = 128
SUBLANES = 8
ROW_TILE = 256
MLSTM_CHUNK = 256
SAMPLE_CHUNK = 128
VMEM_LIMIT = 56 * 1024 * 1024


def _resident(shape):
    nd = len(shape)
    return pl.BlockSpec(shape, lambda *_: (0,) * nd, pipeline_mode=pl.Buffered(1))


def _params(sem):
    return pltpu.CompilerParams(dimension_semantics=sem, vmem_limit_bytes=VMEM_LIMIT)


def _split_bf16(x, pieces):
    out = []
    r = x
    for _ in range(pieces):
        p = r.astype(BF16)
        out.append(p)
        r = r - p.astype(F32)
    return out


def _dot_nt(a, b):
    return lax.dot_general(a, b, (((1,), (1,)), ((), ())), preferred_element_type=F32)


def _dot_tn(a, b):
    return lax.dot_general(a, b, (((0,), (0,)), ((), ())), preferred_element_type=F32)


def _dot(a, b):
    return jnp.dot(a, b, preferred_element_type=F32)


def _ada_kernel(c_ref, w_ref, b_ref, o_ref):
    c = c_ref[...]
    s = (c * jax.nn.sigmoid(c)).astype(BF16)
    o_ref[...] = _dot(s, w_ref[...].astype(BF16)) + b_ref[...]


def _ada(c, w_ada, b_ada):
    n, d = c.shape
    d_out = w_ada.shape[1]
    tn = d
    return pl.pallas_call(
        _ada_kernel,
        out_shape=jax.ShapeDtypeStruct((n, d_out), F32),
        grid=(d_out // tn,),
        in_specs=[pl.BlockSpec((n, d), lambda j: (0, 0)),
                  pl.BlockSpec((d, tn), lambda j: (0, j)),
                  pl.BlockSpec((1, tn), lambda j: (0, j))],
        out_specs=pl.BlockSpec((n, tn), lambda j: (0, j)),
        compiler_params=_params(("arbitrary",)),
        name="ada_mod",
    )(c, w_ada, b_ada.reshape(1, d_out))


def _inproj_kernel(x_ref, g_ref, sh_ref, sc_ref, cos_ref, sin_ref, w_ref, wg_ref, bg_ref,
                   q_ref, kf_ref, kb_ref, vf_ref, vb_ref, qkb_ref, vbb_ref, sga_ref, sgob_ref,
                   gc_ref, gr_ref, km_ref):
    tm, d = x_ref.shape
    wa = N_HEADS_A * HEAD_DIM_A
    x = x_ref[...]
    ms = jnp.mean(x * x, axis=-1, keepdims=True)
    xn = x * lax.rsqrt(ms + RMS_EPS) * g_ref[...]
    h = (xn * (1.0 + sc_ref[...]) + sh_ref[...]).astype(BF16)
    cos = cos_ref[...]
    sin = sin_ref[...]

    def proj(c):
        return _dot(h, w_ref[:, c * wa:(c + 1) * wa])

    def rope(a, hh):
        xh = a[:, hh * HEAD_DIM_A:(hh + 1) * HEAD_DIM_A]
        return xh * cos + pltpu.roll(xh, HEAD_DIM_A // 2, axis=1) * sin

    qscale = HEAD_DIM_A ** -0.5
    a = proj(0)
    for hh in range(N_HEADS_A):
        sl = slice(hh * HEAD_DIM_A, (hh + 1) * HEAD_DIM_A)
        q_ref[:, sl] = (rope(a, hh) * qscale).astype(BF16)
    a = proj(1)
    for hh in range(N_HEADS_A):
        sl = slice(hh * HEAD_DIM_A, (hh + 1) * HEAD_DIM_A)
        kr = rope(a, hh)
        kf_ref[:, sl] = kr
        kb_ref[:, sl] = kr.astype(BF16)
        for g in range(tm // MOBA_BLOCK):
            km_ref[g, :, sl] = jnp.mean(kr[g * MOBA_BLOCK:(g + 1) * MOBA_BLOCK], axis=0, keepdims=True)
    a = proj(2)
    vf_ref[...] = a
    vb_ref[...] = a.astype(BF16)
    a = proj(3)
    wqk = N_HEADS_B * DK_B
    qkb_ref[:, :wqk] = a[:, :wqk].astype(BF16)
    qkb_ref[:, wqk:] = (a[:, wqk:] * (DK_B ** -0.5)).astype(BF16)
    vbb_ref[...] = proj(4).astype(BF16)
    so = jax.nn.sigmoid(proj(5))
    sga_ref[...] = jax.nn.sigmoid(proj(6)).astype(BF16)
    sgob_ref[...] = (jax.nn.sigmoid(proj(7)) * so).astype(BF16)
    g = _dot(h, wg_ref[...]) + bg_ref[...]
    lane = lax.broadcasted_iota(jnp.int32, g.shape, 1)
    logsig = jnp.minimum(g, 0.0) - jnp.log(1.0 + jnp.exp(-jnp.abs(g)))
    g = jnp.where(lane >= N_HEADS_B, logsig, g)
    gc_ref[...] = g
    gr_ref[...] = g.T[:2 * N_HEADS_B, :]


def _inproj(x, g_mix, shift, scale, cos2, sin2, w_main, w_gate, b_gate, *, rows_per_mod, rows_per_pos):
    n, d = x.shape
    tm = ROW_TILE
    assert n % tm == 0 and tm % MOBA_BLOCK == 0
    wa = N_HEADS_A * HEAD_DIM_A
    if rows_per_mod:
        assert rows_per_mod % tm == 0
        mod_spec = pl.BlockSpec((None, 1, d), lambda i: (i // (rows_per_mod // tm), 0, 0))
    else:
        mod_spec = pl.BlockSpec((tm, d), lambda i: (i, 0))
    assert rows_per_pos % tm == 0
    pos_spec = pl.BlockSpec((tm, HEAD_DIM_A), lambda i: (i % (rows_per_pos // tm), 0))
    row = lambda w: pl.BlockSpec((tm, w), lambda i: (i, 0))
    nblk = tm // MOBA_BLOCK
    out_shape = (
        jax.ShapeDtypeStruct((n, wa), BF16),
        jax.ShapeDtypeStruct((n, wa), F32),
        jax.ShapeDtypeStruct((n, wa), BF16),
        jax.ShapeDtypeStruct((n, wa), F32),
        jax.ShapeDtypeStruct((n, wa), BF16),
        jax.ShapeDtypeStruct((n, 2 * N_HEADS_B * DK_B), BF16),
        jax.ShapeDtypeStruct((n, N_HEADS_B * DV_B), BF16),
        jax.ShapeDtypeStruct((n, wa), BF16),
        jax.ShapeDtypeStruct((n, N_HEADS_B * DV_B), BF16),
        jax.ShapeDtypeStruct((n, LANES), F32),
        jax.ShapeDtypeStruct((2 * N_HEADS_B, n), F32),
        jax.ShapeDtypeStruct((n // MOBA_BLOCK, 1, wa), F32),
    )
    out_specs = (row(wa), row(wa), row(wa), row(wa), row(wa), row(2 * N_HEADS_B * DK_B), row(N_HEADS_B * DV_B),
                 row(wa), row(N_HEADS_B * DV_B), row(LANES),
                 pl.BlockSpec((2 * N_HEADS_B, tm), lambda i: (0, i)),
                 pl.BlockSpec((nblk, 1, wa), lambda i: (i, 0, 0)))
    return pl.pallas_call(
        _inproj_kernel,
        out_shape=out_shape,
        grid=(n // tm,),
        in_specs=[row(d), _resident((1, d)), mod_spec, mod_spec, pos_spec, pos_spec,
                  _resident(w_main.shape), _resident(w_gate.shape), _resident((1, LANES))],
        out_specs=out_specs,
        compiler_params=_params(("parallel",)),
        name="in_proj",
    )(x, g_mix, shift, scale, cos2, sin2, w_main, w_gate, b_gate)


def _topk_mask(gate, valid, n_blocks):
    blk = lax.broadcasted_iota(jnp.int32, gate.shape, 1).astype(F32)
    g = jnp.where(valid, gate, -jnp.inf)
    sel = jnp.zeros(gate.shape, F32)
    for _ in range(MOBA_TOPK):
        mx = jnp.max(g, axis=1, keepdims=True)
        idx = jnp.min(jnp.where(g == mx, blk, float(n_blocks)), axis=1, keepdims=True)
        pick = blk == idx
        sel = jnp.where(pick, jnp.where(valid, 1.0, 0.0), sel)
        g = jnp.where(pick, -jnp.inf, g)
    return sel


def _gate_scores(q, kmean):
    hi, lo = _split_bf16(kmean, 2)
    return _dot_nt(q, hi) + _dot_nt(q, lo)


def _moba_prompt_kernel(q_ref, k_ref, v_ref, km_ref, o_ref):
    i = pl.program_id(2)
    tq = q_ref.shape[0]
    n_blocks = km_ref.shape[0]
    q = q_ref[...]
    gate = _gate_scores(q, km_ref[...])
    blk = lax.broadcasted_iota(jnp.int32, gate.shape, 1)
    sel = _topk_mask(gate, blk < i, n_blocks)
    bias = jnp.where(sel > 0.5, 0.0, MASKED).astype(BF16)

    start = pl.multiple_of(i * tq, tq)
    s = _dot_nt(q, k_ref[pl.ds(start, tq), :])
    r = lax.broadcasted_iota(jnp.int32, s.shape, 0)
    c = lax.broadcasted_iota(jnp.int32, s.shape, 1)
    s = jnp.where(c <= r, s, MASKED)
    m = jnp.max(s, axis=1, keepdims=True)
    p = jnp.exp(s - m)
    l = jnp.sum(p, axis=1, keepdims=True)
    acc = _dot(p.astype(BF16), v_ref[pl.ds(start, tq), :])

    blk_row = lax.broadcasted_iota(jnp.int32, (n_blocks, tq), 0)

    def body(j, carry):
        m, l, acc = carry
        st = pl.multiple_of(j * tq, tq)
        onehot = jnp.where(blk_row == j, 1.0, 0.0).astype(BF16)
        s = _dot_nt(q, k_ref[pl.ds(st, tq), :]) + _dot(bias, onehot)
        m_new = jnp.maximum(m, jnp.max(s, axis=1, keepdims=True))
        alpha = jnp.exp(m - m_new)
        p = jnp.exp(s - m_new)
        l = alpha * l + jnp.sum(p, axis=1, keepdims=True)
        acc = alpha * acc + _dot(p.astype(BF16), v_ref[pl.ds(st, tq), :])
        return m_new, l, acc

    m, l, acc = lax.fori_loop(0, i, body, (m, l, acc))
    o_ref[...] = (acc / l).astype(o_ref.dtype)


def _moba_prompt(q, k, v, kmean):
    b, t, w = q.shape
    tq = MOBA_BLOCK
    hd = HEAD_DIM_A
    nb = t // tq
    return pl.pallas_call(
        _moba_prompt_kernel,
        out_shape=jax.ShapeDtypeStruct((b, t, w), BF16),
        grid=(b, w // hd, nb),
        in_specs=[pl.BlockSpec((None, tq, hd), lambda bb, hh, i: (bb, i, hh)),
                  pl.BlockSpec((None, t, hd), lambda bb, hh, i: (bb, 0, hh)),
                  pl.BlockSpec((None, t, hd), lambda bb, hh, i: (bb, 0, hh)),
                  pl.BlockSpec((None, nb, hd), lambda bb, hh, i: (bb, 0, hh))],
        out_specs=pl.BlockSpec((None, tq, hd), lambda bb, hh, i: (bb, i, hh)),
        compiler_params=_params(("parallel", "parallel", "arbitrary")),
        name="moba_prompt",
    )(q, k, v, kmean)


def _mlstm_kernel(qk_ref, v_ref, gc_ref, gr_ref, c0_ref, n0_ref, m0_ref,
                  h_ref, c_out, n_out, m_out, c_s, n_s, m_s, *, valid_rows):
    ci = pl.program_id(1)
    L = qk_ref.shape[0]
    wqk = N_HEADS_B * DK_B

    @pl.when(ci == 0)
    def _():
        c_s[...] = c0_ref[...]
        n_s[...] = n0_ref[...]
        m_s[...] = m0_ref[...]

    gc = gc_ref[...]
    gr = gr_ref[...]
    if valid_rows < L:
        rr = lax.broadcasted_iota(jnp.int32, gc.shape, 0)
        ln = lax.broadcasted_iota(jnp.int32, gc.shape, 1)
        gc = jnp.where(rr < valid_rows, gc, jnp.where(ln < N_HEADS_B, MASKED, 0.0))
        cc = lax.broadcasted_iota(jnp.int32, gr.shape, 1)
        rw = lax.broadcasted_iota(jnp.int32, gr.shape, 0)
        gr = jnp.where(cc < valid_rows, gr, jnp.where(rw < N_HEADS_B, MASKED, 0.0))

    r = lax.broadcasted_iota(jnp.int32, (L, L), 0)
    c = lax.broadcasted_iota(jnp.int32, (L, L), 1)
    causal = c <= r
    tri = jnp.where(causal, 1.0, 0.0).astype(BF16)
    triu = jnp.where(r <= c, 1.0, 0.0).astype(BF16)
    bc_all = sum(_dot(tri, p) for p in _split_bf16(gc, 3))
    br_all = sum(_dot(p, triu) for p in _split_bf16(gr, 3))

    for hh in range(N_HEADS_B):
        q = qk_ref[:, hh * DK_B:(hh + 1) * DK_B]
        k = qk_ref[:, wqk + hh * DK_B:wqk + (hh + 1) * DK_B]
        v = v_ref[:, hh * DV_B:(hh + 1) * DV_B]
        i_col = gc[:, hh:hh + 1]
        b_col = bc_all[:, N_HEADS_B + hh:N_HEADS_B + hh + 1]
        i_row = gr[hh:hh + 1, :]
        b_row = br_all[N_HEADS_B + hh:N_HEADS_B + hh + 1, :]
        C = c_s[hh]
        n = n_s[hh]
        m_prev = m_s[hh][:, :1]

        dmat = jnp.where(causal, b_col - b_row + i_row, MASKED)
        inter = b_col + m_prev
        m_t = jnp.maximum(inter, jnp.max(dmat, axis=1, keepdims=True))
        w = jnp.exp(dmat - m_t)
        a_inter = jnp.exp(inter - m_t)
        qk = _dot_nt(q, k) * w
        num = a_inter * _dot_nt(q, C.astype(BF16)) + _dot(qk.astype(BF16), v)
        qn = jnp.sum(q.astype(F32) * n, axis=1, keepdims=True)
        den = a_inter * qn + jnp.sum(qk, axis=1, keepdims=True)
        hout = num / jnp.maximum(jnp.abs(den), jnp.exp(-m_t))
        h_ref[:, hh * DV_B:(hh + 1) * DV_B] = hout.astype(h_ref.dtype)

        b_last = b_col[L - 1:L, :]
        g_col = b_last - b_col + i_col
        m_new = jnp.maximum(b_last + m_prev, jnp.max(g_col, axis=0, keepdims=True))
        decay = jnp.exp(b_last + m_prev - m_new)
        ws = jnp.exp(g_col - m_new)
        kw = k.astype(F32) * ws
        c_s[hh] = decay * C + _dot_tn(v, kw.astype(BF16))
        n_s[hh] = decay * n + jnp.sum(kw, axis=0, keepdims=True)
        m_s[hh] = jnp.broadcast_to(m_new, m_s.shape[1:])

    @pl.when(ci == pl.num_programs(1) - 1)
    def _():
        c_out[...] = c_s[...]
        n_out[...] = n_s[...]
        m_out[...] = m_s[...]


def _mlstm(qkb, vb, gates_col, gates_row, c0, n0, m0, *, chunk, valid_rows):
    b, t, _ = qkb.shape
    nc = t // chunk
    hb = N_HEADS_B
    st4 = lambda shp: pl.BlockSpec((None,) + shp, lambda bb, ci: (bb, 0, 0, 0))
    kern = functools.partial(_mlstm_kernel, valid_rows=valid_rows)
    return pl.pallas_call(
        kern,
        out_shape=(jax.ShapeDtypeStruct((b, t, hb * DV_B), BF16),
                   jax.ShapeDtypeStruct(c0.shape, F32),
                   jax.ShapeDtypeStruct(n0.shape, F32),
                   jax.ShapeDtypeStruct(m0.shape, F32)),
        grid=(b, nc),
        in_specs=[pl.BlockSpec((None, chunk, qkb.shape[2]), lambda bb, ci: (bb, ci, 0)),
                  pl.BlockSpec((None, chunk, vb.shape[2]), lambda bb, ci: (bb, ci, 0)),
                  pl.BlockSpec((None, chunk, LANES), lambda bb, ci: (bb, ci, 0)),
                  pl.BlockSpec((None, 2 * hb, chunk), lambda bb, ci: (bb * nc + ci, 0, 0)),
                  st4((hb, DV_B, DK_B)), st4((hb, 1, DK_B)), st4((hb, 1, LANES))],
        out_specs=(pl.BlockSpec((None, chunk, hb * DV_B), lambda bb, ci: (bb, ci, 0)),
                   st4((hb, DV_B, DK_B)), st4((hb, 1, DK_B)), st4((hb, 1, LANES))),
        scratch_shapes=[pltpu.VMEM((hb, DV_B, DK_B), F32),
                        pltpu.VMEM((hb, 1, DK_B), F32),
                        pltpu.VMEM((hb, 1, LANES), F32)],
        compiler_params=_params(("parallel", "arbitrary")),
        name="mlstm",
    )(qkb, vb, gates_col, gates_row, c0, n0, m0)


def _ffn_kernel(x_ref, ya_ref, hb_ref, sga_ref, sgob_ref, gtm_ref, shf_ref, scf_ref, gtf_ref,
                gffn_ref, gfin_ref, wout_ref, wup_ref, wconv_ref, bconv_ref, wdown_ref, *rest, seq_rows):
    if seq_rows:
        fix1_ref, fix2_ref, y_ref, a_ref = rest
    else:
        y_ref, tail_ref, carry = rest
    tm = x_ref.shape[0]
    dff = wdown_ref.shape[0]
    merged = (sga_ref[...].astype(F32) * ya_ref[...].astype(F32)
              + sgob_ref[...].astype(F32) * hb_ref[...].astype(F32))
    x1 = x_ref[...] + gtm_ref[...] * _dot(merged.astype(BF16), wout_ref[...])
    ms = jnp.mean(x1 * x1, axis=-1, keepdims=True)
    xn = x1 * lax.rsqrt(ms + RMS_EPS) * gffn_ref[...]
    h = (xn * (1.0 + scf_ref[...]) + shf_ref[...]).astype(BF16)
    a = _dot(h, wup_ref[:, :dff])
    gb = _dot(h, wup_ref[:, dff:])

    row = lax.broadcasted_iota(jnp.int32, a.shape, 0)
    r1 = pltpu.roll(a, 1, axis=0)
    r2 = pltpu.roll(a, 2, axis=0)
    if seq_rows:
        assert seq_rows & (seq_rows - 1) == 0
        t = row & (seq_rows - 1)
        a1 = jnp.where(t >= 1, r1, fix1_ref[...])
        a2 = jnp.where(t >= 2, r2, fix2_ref[...])
        a_ref[...] = a
    else:
        @pl.when(pl.program_id(1) == 0)
        def _():
            carry[...] = jnp.zeros_like(carry)
        prev1 = carry[SUBLANES - 1:SUBLANES, :]
        prev2 = carry[SUBLANES - 2:SUBLANES - 1, :]
        a1 = jnp.where(row == 0, prev1, r1)
        a2 = jnp.where(row == 0, prev2, jnp.where(row == 1, prev1, r2))
        tail = a[tm - SUBLANES:, :]
        carry[...] = tail
        tail_ref[...] = tail
    ac = bconv_ref[...] + a2 * wconv_ref[0:1, :] + a1 * wconv_ref[1:2, :] + a * wconv_ref[2:3, :]
    gelu = 0.5 * ac * (1.0 + jnp.tanh(math.sqrt(2.0 / math.pi) * (ac + 0.044715 * (ac * ac * ac))))
    f = _dot((gelu * gb).astype(BF16), wdown_ref[...])
    x2 = x1 + gtf_ref[...] * f
    ms2 = jnp.mean(x2 * x2, axis=-1, keepdims=True)
    y_ref[...] = x2 * lax.rsqrt(ms2 + RMS_EPS) * gfin_ref[...]


def _ffn(x, ya, hb, sga, sgob, gtm, shf, scf, gtf, g_ffn, g_fin, w_out, w_up, w_conv, b_conv, w_down,
         *, batch, seq_rows=0, fix1=None, fix2=None):
    n, d = x.shape
    dff = w_down.shape[0]
    tm = ROW_TILE
    if seq_rows:
        assert n == tm
        grid = (1, 1)
        mod_spec = pl.BlockSpec((tm, d), lambda bb, i: (0, 0))
        nt = 1
    else:
        nt = n // batch // tm
        grid = (batch, nt)
        mod_spec = pl.BlockSpec((None, 1, d), lambda bb, i: (bb, 0, 0))
    row = lambda w: pl.BlockSpec((tm, w), lambda bb, i: (bb * nt + i, 0))
    in_specs = [row(d), row(d), row(d), row(d), row(d), mod_spec, mod_spec, mod_spec, mod_spec,
                _resident((1, d)), _resident((1, d)), _resident(w_out.shape), _resident(w_up.shape),
                _resident(w_conv.shape), _resident((1, dff)), _resident(w_down.shape)]
    args = [x, ya, hb, sga, sgob, gtm, shf, scf, gtf, g_ffn, g_fin, w_out, w_up, w_conv, b_conv, w_down]
    if seq_rows:
        in_specs += [row(dff), row(dff)]
        args += [fix1, fix2]
        out_shape = (jax.ShapeDtypeStruct((n, d), F32), jax.ShapeDtypeStruct((n, dff), F32))
        out_specs = (row(d), row(dff))
        scratch = []
    else:
        out_shape = (jax.ShapeDtypeStruct((n, d), F32), jax.ShapeDtypeStruct((batch, SUBLANES, dff), F32))
        out_specs = (row(d), pl.BlockSpec((None, SUBLANES, dff), lambda bb, i: (bb, 0, 0)))
        scratch = [pltpu.VMEM((SUBLANES, dff), F32)]
    return pl.pallas_call(
        functools.partial(_ffn_kernel, seq_rows=seq_rows),
        out_shape=out_shape,
        grid=grid,
        in_specs=in_specs,
        out_specs=out_specs,
        scratch_shapes=scratch,
        compiler_params=_params(("parallel", "arbitrary")),
        name="merge_ffn",
    )(*args)


def _kmean_kernel(pt_ref, *refs):
    pages = refs[:-1]
    o_ref = refs[-1]
    per_blk = MOBA_BLOCK // pages[0].shape[0]
    for r in range(len(pages) // per_blk):
        s = jnp.sum(pages[r * per_blk][...], axis=0)
        for p in range(1, per_blk):
            s = s + jnp.sum(pages[r * per_blk + p][...], axis=0)
        o_ref[r] = s * (1.0 / MOBA_BLOCK)


def _cache_kmean(cache_k, page_table, *, pages_per_step=8):
    _, page, nh, hd = cache_k.shape
    b, n_pages = page_table.shape
    assert MOBA_BLOCK % page == 0 and n_pages % pages_per_step == 0
    per_blk = MOBA_BLOCK // page
    blk_per_step = pages_per_step // per_blk
    steps = n_pages // pages_per_step
    in_specs = [pl.BlockSpec((None, page, nh, hd), functools.partial(
        lambda bb, g, pt, p: (pt[bb, g * pages_per_step + p], 0, 0, 0), p=p)) for p in range(pages_per_step)]
    return pl.pallas_call(
        _kmean_kernel,
        out_shape=jax.ShapeDtypeStruct((b, n_pages // per_blk, nh, hd), F32),
        grid_spec=pltpu.PrefetchScalarGridSpec(
            num_scalar_prefetch=1,
            grid=(b, steps),
            in_specs=in_specs,
            out_specs=pl.BlockSpec((None, blk_per_step, nh, hd), lambda bb, g, pt: (bb, g, 0, 0))),
        compiler_params=_params(("parallel", "arbitrary")),
        name="cache_kmean",
    )(page_table, *([cache_k] * pages_per_step))


def _select_kernel(q_ref, km_ref, o_ref):
    n_blocks = km_ref.shape[0]
    ts = q_ref.shape[0]
    lane = lax.broadcasted_iota(jnp.int32, (ts, LANES), 1)
    for hh in range(N_HEADS_A):
        sl = slice(hh * HEAD_DIM_A, (hh + 1) * HEAD_DIM_A)
        gate = _gate_scores(q_ref[:, sl].astype(BF16), km_ref[:, hh, :])
        g = gate
        blk = lax.broadcasted_iota(jnp.int32, gate.shape, 1).astype(F32)
        out = jnp.zeros((ts, LANES), F32)
        for r in range(MOBA_TOPK):
            mx = jnp.max(g, axis=1, keepdims=True)
            idx = jnp.min(jnp.where(g == mx, blk, float(n_blocks)), axis=1, keepdims=True)
            out = jnp.where(lane == r, idx, out)
            g = jnp.where(blk == idx, -jnp.inf, g)
        o_ref[hh] = out.astype(jnp.int32)


def _select_blocks(q, kmean):
    b, ts, w = q.shape
    _, nb, nh, hd = kmean.shape
    return pl.pallas_call(
        _select_kernel,
        out_shape=jax.ShapeDtypeStruct((b, N_HEADS_A, ts, LANES), jnp.int32),
        grid=(b,),
        in_specs=[pl.BlockSpec((None, ts, w), lambda bb: (bb, 0, 0)),
                  pl.BlockSpec((None, nb, nh, hd), lambda bb: (bb, 0, 0, 0))],
        out_specs=pl.BlockSpec((None, N_HEADS_A, ts, LANES), lambda bb: (bb, 0, 0, 0)),
        compiler_params=_params(("parallel",)),
        name="select_blocks",
    )(q, kmean)


def _moba_sample_kernel(sel_ref, pt_ref, q_ref, kn_ref, vn_ref, ck_ref, cv_ref, o_ref, kbuf, vbuf, sem,
                        *, n_pages, per_blk):
    b = pl.program_id(0)
    hh = pl.program_id(1)
    ts = q_ref.shape[0]
    page = kbuf.shape[1]
    nsel = MOBA_TOPK * per_blk

    def copies(t, r, p):
        blk = sel_ref[((b * N_HEADS_A + hh) * ts + t) * MOBA_TOPK + r]
        pg = pt_ref[b * n_pages + blk * per_blk + p]
        slot = (t * MOBA_TOPK + r) * per_blk + p
        return (pltpu.make_async_copy(ck_ref.at[pg, :, hh, :], kbuf.at[slot], sem.at[0]),
                pltpu.make_async_copy(cv_ref.at[pg, :, hh, :], vbuf.at[slot], sem.at[1]))

    for t in range(ts):
        for r in range(MOBA_TOPK):
            for p in range(per_blk):
                ck, cv = copies(t, r, p)
                ck.start()
                cv.start()
    for t in range(ts):
        for r in range(MOBA_TOPK):
            for p in range(per_blk):
                ck, cv = copies(t, r, p)
                ck.wait()
                cv.wait()

    qf = q_ref[...]
    q = qf.astype(BF16)
    kn = kn_ref[...]
    vn = vn_ref[...]
    jrow = lax.broadcasted_iota(jnp.int32, (ts, LANES), 0)
    for t in range(ts):
        qrep = jnp.broadcast_to(q[t:t + 1, :], (LANES, q.shape[1]))
        kc = kbuf[t * nsel:(t + 1) * nsel].reshape(nsel * page, -1).astype(BF16)
        vc = vbuf[t * nsel:(t + 1) * nsel].reshape(nsel * page, -1)
        s = _dot_nt(kc, qrep)
        s_own = jnp.where(jrow <= t, jnp.sum(kn * qf[t:t + 1, :], axis=1, keepdims=True), MASKED)
        m = jnp.maximum(jnp.max(s, axis=0, keepdims=True), jnp.max(s_own, axis=0, keepdims=True))
        p = jnp.exp(s - m)
        p_own = jnp.exp(s_own - m)
        l = jnp.sum(p, axis=0, keepdims=True) + jnp.sum(p_own, axis=0, keepdims=True)
        o = jnp.sum(p * vc, axis=0, keepdims=True) + jnp.sum(p_own * vn, axis=0, keepdims=True)
        o_ref[t:t + 1, :] = o / l


def _moba_sample(q, k_new, v_new, cache_k, cache_v, page_table, sel):
    b, n_pages = page_table.shape
    ts = q.shape[0] // b
    _, page, nh, hd = cache_k.shape
    per_blk = MOBA_BLOCK // page
    nslots = ts * MOBA_TOPK * per_blk
    blk = pl.BlockSpec((ts, hd), lambda bb, hh, *_: (bb, hh))
    kern = functools.partial(_moba_sample_kernel, n_pages=n_pages, per_blk=per_blk)
    return pl.pallas_call(
        kern,
        out_shape=jax.ShapeDtypeStruct(q.shape, F32),
        grid_spec=pltpu.PrefetchScalarGridSpec(
            num_scalar_prefetch=2,
            grid=(b, nh),
            in_specs=[blk, blk, blk, pl.BlockSpec(memory_space=pl.ANY), pl.BlockSpec(memory_space=pl.ANY)],
            out_specs=blk,
            scratch_shapes=[pltpu.VMEM((nslots, page, hd), F32), pltpu.VMEM((nslots, page, hd), F32),
                            pltpu.SemaphoreType.DMA((2,))]),
        compiler_params=_params(("arbitrary", "arbitrary")),
        name="moba_sample",
    )(sel.reshape(-1), page_table.reshape(-1), q, k_new, v_new, cache_k, cache_v)


def _rope_tables(pos):
    half = HEAD_DIM_A // 2
    inv = ROPE_THETA ** (-jnp.arange(half, dtype=F32) / half)
    ang = pos.astype(F32)[:, None] * inv[None, :]
    cos, sin = jnp.cos(ang), jnp.sin(ang)
    return jnp.concatenate([cos, cos], axis=1), jnp.concatenate([-sin, sin], axis=1)


def _pack_w_in(w_in):
    d = w_in.shape[0]
    a = 3 * N_HEADS_A * HEAD_DIM_A + 2 * N_HEADS_B * DK_B + 2 * N_HEADS_B * DV_B
    ng = 2 * N_HEADS_B
    w_main = jnp.concatenate([w_in[:, :a], w_in[:, a + ng:]], axis=1).astype(BF16)
    w_gate = jnp.zeros((d, LANES), BF16).at[:, :ng].set(w_in[:, a:a + ng].astype(BF16))
    return w_main, w_gate


def _layer(x, c_mod, pos, w, *, sample=None):
    (g_mix, w_main, w_gate, b_gate, w_out, g_ffn, w_up, w_conv, b_conv, w_down, g_fin) = w
    b, t, d = x.shape
    n = b * t
    wa = N_HEADS_A * HEAD_DIM_A
    dff = w_down.shape[0]
    hb = N_HEADS_B
    sh_m, sc_m, gt_m, sh_f, sc_f, gt_f = jnp.split(c_mod, 6, axis=-1)
    cos2, sin2 = _rope_tables(pos)
    x2d = x.reshape(n, d)

    if sample is None:
        per_seq = lambda a: a.reshape(b, 1, d)
        q, kf, kb16, vf, vb16, qkb, vbb, sga, sgob, gcol, grow, kmean = _inproj(
            x2d, g_mix, per_seq(sh_m), per_seq(sc_m), cos2, sin2, w_main, w_gate, b_gate,
            rows_per_mod=t, rows_per_pos=t)
        ya = _moba_prompt(q.reshape(b, t, wa), kb16.reshape(b, t, wa), vb16.reshape(b, t, wa),
                          kmean.reshape(b, t // MOBA_BLOCK, wa)).reshape(n, wa)
        chunk = MLSTM_CHUNK
        nc = t // chunk
        grow3 = grow.reshape(2 * hb, b * nc, chunk).transpose(1, 0, 2)
        c0 = jnp.zeros((b, hb, DV_B, DK_B), F32)
        n0 = jnp.zeros((b, hb, 1, DK_B), F32)
        m0 = jnp.zeros((b, hb, 1, LANES), F32)
        hbo, c_new, n_new, m_new = _mlstm(qkb.reshape(b, t, -1), vbb.reshape(b, t, -1), gcol.reshape(b, t, LANES),
                                          grow3, c0, n0, m0, chunk=chunk, valid_rows=chunk)
        y, tail = _ffn(x2d, ya, hbo.reshape(n, -1), sga, sgob, per_seq(gt_m), per_seq(sh_f), per_seq(sc_f),
                       per_seq(gt_f), g_ffn, g_fin, w_out, w_up, w_conv, b_conv, w_down, batch=b)
        conv_new = tail[:, SUBLANES - (CONV_W - 1):, :]
    else:
        cache_k, cache_v, page_table, c_st, n_st, m_st, conv0 = sample
        assert n == ROW_TILE and t >= CONV_W - 1
        past = page_table.shape[1] * cache_k.shape[1]
        assert past % MOBA_BLOCK == 0 and t <= MOBA_BLOCK
        per_row = lambda a: jnp.repeat(a, t, axis=0)
        cos2, sin2 = jnp.tile(cos2, (b, 1)), jnp.tile(sin2, (b, 1))
        q, kf, kb16, vf, vb16, qkb, vbb, sga, sgob, gcol, grow, _ = _inproj(
            x2d, g_mix, per_row(sh_m), per_row(sc_m), cos2, sin2, w_main, w_gate, b_gate,
            rows_per_mod=0, rows_per_pos=n)
        kmean = _cache_kmean(cache_k, page_table)
        qf = q.astype(F32)
        sel = _select_blocks(qf.reshape(b, t, wa), kmean)[..., :MOBA_TOPK]
        ya = _moba_sample(qf, kf, vf, cache_k, cache_v, page_table, sel).astype(BF16)
        chunk = SAMPLE_CHUNK
        pad = lambda a: jnp.pad(a.reshape(b, t, -1), ((0, 0), (0, chunk - t), (0, 0)))
        grow3 = jnp.pad(grow.reshape(2 * hb, b, t).transpose(1, 0, 2), ((0, 0), (0, 0), (0, chunk - t)))
        m0 = jnp.broadcast_to(m_st[:, :, None, None], (b, hb, 1, LANES))
        hbo, c_new, n_new, m_new = _mlstm(pad(qkb), pad(vbb), pad(gcol), grow3, c_st, n_st[:, :, None, :], m0,
                                          chunk=chunk, valid_rows=t)
        hbo = hbo[:, :t].reshape(n, -1)
        zero = jnp.zeros((b, t - 2, dff), F32)
        fix1 = jnp.concatenate([conv0[:, 1:2], zero, zero[:, :1]], axis=1).reshape(n, dff)
        fix2 = jnp.concatenate([conv0, zero], axis=1).reshape(n, dff)
        y, a_full = _ffn(x2d, ya, hbo, sga, sgob, per_row(gt_m), per_row(sh_f), per_row(sc_f), per_row(gt_f),
                         g_ffn, g_fin, w_out, w_up, w_conv, b_conv, w_down, batch=b, seq_rows=t,
                         fix1=fix1, fix2=fix2)
        conv_new = a_full.reshape(b, t, dff)[:, t - (CONV_W - 1):]
    return (y.reshape(b, t, d), kf.reshape(b, t, N_HEADS_A, HEAD_DIM_A), vf.reshape(b, t, N_HEADS_A, HEAD_DIM_A),
            c_new, n_new[:, :, 0, :], m_new[:, :, 0, 0], conv_new)


def kernel(x_prompt, x_sample, cache_k, cache_v, page_table, state_C, state_n, state_m, state_conv,
           c_prompt, c_sample, w_ada, b_ada, g_norm_mix, w_in, b_gates, w_out, g_norm_ffn,
           w_up, w_conv, b_conv, w_down, g_norm_final):
    depth = w_in.shape[0]
    assert depth == 1
    bp, tp, d = x_prompt.shape
    bs, ts, _ = x_sample.shape
    past = page_table.shape[1] * cache_k.shape[2]
    dl = 0
    mod = _ada(jnp.concatenate([c_prompt, c_sample], axis=0), w_ada[dl], b_ada[dl])
    w_main, w_gate = _pack_w_in(w_in[dl])
    b_gate = jnp.zeros((1, LANES), F32).at[0, :2 * N_HEADS_B].set(b_gates[dl])
    w = (g_norm_mix[dl].reshape(1, d), w_main, w_gate, b_gate, w_out[dl].astype(BF16),
         g_norm_ffn[dl].reshape(1, d), w_up[dl].astype(BF16), w_conv[dl], b_conv[dl].reshape(1, -1),
         w_down[dl].astype(BF16), g_norm_final.reshape(1, d))
    yp, kp, vp, cp, np_, mp, cvp = _layer(x_prompt, mod[:bp], jnp.arange(tp, dtype=jnp.int32), w)
    ys, ks, vs, cs, ns, ms, cvs = _layer(
        x_sample, mod[bp:], past + jnp.arange(ts, dtype=jnp.int32), w,
        sample=(cache_k[dl], cache_v[dl], page_table, state_C[dl], state_n[dl], state_m[dl], state_conv[dl]))
    st = lambda a: a[None]
    return (yp, ys, st(kp), st(vp), st(ks), st(vs), st(cp), st(np_), st(mp), st(cs), st(ns), st(ms),
            st(cvp), st(cvs))
```

```python
import functools
import math

import jax
import jax.numpy as jnp
from jax import lax
from jax.experimental import pallas as pl
from jax.experimental.pallas import tpu as pltpu

F32 = jnp.float32
BF16 = jnp.bfloat16

N_HEADS_A = 8
HEAD_DIM_A = 128
MOBA_BLOCK = 256
MOBA_TOPK = 3
ROPE_THETA = 10000.0
N_HEADS_B = 4
DK_B = 128
DV_B = 256
CONV_W = 3
RMS_EPS = 1e-6
MASKED = -1e30
LANES = 128
SUBLANES = 8
ROW_TILE = 256
MLSTM_CHUNK = 256
SAMPLE_CHUNK = 128
VMEM_LIMIT = 56 * 1024 * 1024


def _resident(shape):
    nd = len(shape)
    return pl.BlockSpec(shape, lambda *_: (0,) * nd, pipeline_mode=pl.Buffered(1))


def _params(sem):
    return pltpu.CompilerParams(dimension_semantics=sem, vmem_limit_bytes=VMEM_LIMIT)


def _split_bf16(x, pieces):
    out = []
    r = x
    for _ in range(pieces):
        p = r.astype(BF16)
        out.append(p)
        r = r - p.astype(F32)
    return out


def _dot_nt(a, b):
    return lax.dot_general(a, b, (((1,), (1,)), ((), ())), preferred_element_type=F32)


def _dot_tn(a, b):
    return lax.dot_general(a, b, (((0,), (0,)), ((), ())), preferred_element_type=F32)


def _dot(a, b):
    return jnp.dot(a, b, preferred_element_type=F32)


def _ada_kernel(c_ref, w_ref, b_ref, o_ref):
    c = c_ref[...]
    s = (c * jax.nn.sigmoid(c)).astype(BF16)
    o_ref[...] = _dot(s, w_ref[...].astype(BF16)) + b_ref[...]


def _ada(c, w_ada, b_ada):
    n, d = c.shape
    d_out = w_ada.shape[1]
    tn = d
    return pl.pallas_call(
        _ada_kernel,
        out_shape=jax.ShapeDtypeStruct((n, d_out), F32),
        grid=(d_out // tn,),
        in_specs=[pl.BlockSpec((n, d), lambda j: (0, 0)),
                  pl.BlockSpec((d, tn), lambda j: (0, j)),
                  pl.BlockSpec((1, tn), lambda j: (0, j))],
        out_specs=pl.BlockSpec((n, tn), lambda j: (0, j)),
        compiler_params=_params(("arbitrary",)),
        name="ada_mod",
    )(c, w_ada, b_ada.reshape(1, d_out))


def _inproj_kernel(x_ref, g_ref, sh_ref, sc_ref, cos_ref, sin_ref, w_ref, wg_ref, bg_ref,
                   q_ref, kf_ref, kb_ref, vf_ref, vb_ref, qkb_ref, vbb_ref, sga_ref, sgob_ref,
                   gc_ref, gr_ref, km_ref):
    tm, d = x_ref.shape
    wa = N_HEADS_A * HEAD_DIM_A
    x = x_ref[...]
    ms = jnp.mean(x * x, axis=-1, keepdims=True)
    xn = x * lax.rsqrt(ms + RMS_EPS) * g_ref[...]
    h = (xn * (1.0 + sc_ref[...]) + sh_ref[...]).astype(BF16)
    cos = cos_ref[...]
    sin = sin_ref[...]

    def proj(c):
        return _dot(h, w_ref[:, c * wa:(c + 1) * wa])

    def rope(a, hh):
        xh = a[:, hh * HEAD_DIM_A:(hh + 1) * HEAD_DIM_A]
        return xh * cos + pltpu.roll(xh, HEAD_DIM_A // 2, axis=1) * sin

    qscale = HEAD_DIM_A ** -0.5 * math.log2(math.e)
    a = proj(0)
    for hh in range(N_HEADS_A):
        sl = slice(hh * HEAD_DIM_A, (hh + 1) * HEAD_DIM_A)
        q_ref[:, sl] = (rope(a, hh) * qscale).astype(BF16)
    a = proj(1)
    for hh in range(N_HEADS_A):
        sl = slice(hh * HEAD_DIM_A, (hh + 1) * HEAD_DIM_A)
        kr = rope(a, hh)
        kf_ref[:, sl] = kr
        kb_ref[:, sl] = kr.astype(BF16)
        for g in range(tm // MOBA_BLOCK):
            km_ref[g, :, sl] = jnp.mean(kr[g * MOBA_BLOCK:(g + 1) * MOBA_BLOCK], axis=0, keepdims=True)
    a = proj(2)
    vf_ref[...] = a
    vb_ref[...] = a.astype(BF16)
    a = proj(3)
    wqk = N_HEADS_B * DK_B
    qkb_ref[:, :wqk] = a[:, :wqk].astype(BF16)
    qkb_ref[:, wqk:] = (a[:, wqk:] * (DK_B ** -0.5)).astype(BF16)
    vbb_ref[...] = proj(4).astype(BF16)
    so = jax.nn.sigmoid(proj(5))
    sga_ref[...] = jax.nn.sigmoid(proj(6)).astype(BF16)
    sgob_ref[...] = (jax.nn.sigmoid(proj(7)) * so).astype(BF16)
    g = _dot(h, wg_ref[...]) + bg_ref[...]
    lane = lax.broadcasted_iota(jnp.int32, g.shape, 1)
    logsig = jnp.minimum(g, 0.0) - jnp.log(1.0 + jnp.exp(-jnp.abs(g)))
    g = jnp.where(lane >= N_HEADS_B, logsig, g)
    gc_ref[...] = g
    gr_ref[...] = g.T[:2 * N_HEADS_B, :]


def _inproj(x, g_mix, shift, scale, cos2, sin2, w_main, w_gate, b_gate, *, rows_per_mod, rows_per_pos):
    n, d = x.shape
    tm = ROW_TILE
    assert n % tm == 0 and tm % MOBA_BLOCK == 0
    wa = N_HEADS_A * HEAD_DIM_A
    if rows_per_mod:
        assert rows_per_mod % tm == 0
        mod_spec = pl.BlockSpec((None, 1, d), lambda i: (i // (rows_per_mod // tm), 0, 0))
    else:
        mod_spec = pl.BlockSpec((tm, d), lambda i: (i, 0))
    assert rows_per_pos % tm == 0
    pos_spec = pl.BlockSpec((tm, HEAD_DIM_A), lambda i: (i % (rows_per_pos // tm), 0))
    row = lambda w: pl.BlockSpec((tm, w), lambda i: (i, 0))
    nblk = tm // MOBA_BLOCK
    out_shape = (
        jax.ShapeDtypeStruct((n, wa), BF16),
        jax.ShapeDtypeStruct((n, wa), F32),
        jax.ShapeDtypeStruct((n, wa), BF16),
        jax.ShapeDtypeStruct((n, wa), F32),
        jax.ShapeDtypeStruct((n, wa), BF16),
        jax.ShapeDtypeStruct((n, 2 * N_HEADS_B * DK_B), BF16),
        jax.ShapeDtypeStruct((n, N_HEADS_B * DV_B), BF16),
        jax.ShapeDtypeStruct((n, wa), BF16),
        jax.ShapeDtypeStruct((n, N_HEADS_B * DV_B), BF16),
        jax.ShapeDtypeStruct((n, LANES), F32),
        jax.ShapeDtypeStruct((2 * N_HEADS_B, n), F32),
        jax.ShapeDtypeStruct((n // MOBA_BLOCK, 1, wa), F32),
    )
    out_specs = (row(wa), row(wa), row(wa), row(wa), row(wa), row(2 * N_HEADS_B * DK_B), row(N_HEADS_B * DV_B),
                 row(wa), row(N_HEADS_B * DV_B), row(LANES),
                 pl.BlockSpec((2 * N_HEADS_B, tm), lambda i: (0, i)),
                 pl.BlockSpec((nblk, 1, wa), lambda i: (i, 0, 0)))
    return pl.pallas_call(
        _inproj_kernel,
        out_shape=out_shape,
        grid=(n // tm,),
        in_specs=[row(d), _resident((1, d)), mod_spec, mod_spec, pos_spec, pos_spec,
                  _resident(w_main.shape), _resident(w_gate.shape), _resident((1, LANES))],
        out_specs=out_specs,
        compiler_params=_params(("parallel",)),
        name="in_proj",
    )(x, g_mix, shift, scale, cos2, sin2, w_main, w_gate, b_gate)


def _topk_mask(gate, valid, axis):
    blk = lax.broadcasted_iota(jnp.int32, gate.shape, axis).astype(F32)
    g = jnp.where(valid, gate, -jnp.inf)
    sel = jnp.zeros(gate.shape, F32)
    for _ in range(MOBA_TOPK):
        mx = jnp.max(g, axis=axis, keepdims=True)
        idx = jnp.min(jnp.where(g == mx, blk, float(gate.shape[axis])), axis=axis, keepdims=True)
        pick = blk == idx
        sel = jnp.where(pick, jnp.where(valid, 1.0, 0.0), sel)
        g = jnp.where(pick, -jnp.inf, g)
    return sel


def _gate_scores(q, kmean):
    hi, lo = _split_bf16(kmean, 2)
    return _dot_nt(q, hi) + _dot_nt(q, lo)


def _gate_scores_t(q, kmean):
    hi, lo = _split_bf16(kmean, 2)
    return _dot_nt(hi, q) + _dot_nt(lo, q)


def _moba_prompt_kernel(q_ref, k_ref, v_ref, km_ref, o_ref, kaug, vext, qa, s0, s1, mx0, mx1, m_ref, acc_ref):
    i = pl.program_id(2)
    tq, hd = q_ref.shape
    t = k_ref.shape[0]

    @pl.when(i == 0)
    def _():
        kaug[:, :hd] = k_ref[...]
        vext[:, :hd] = v_ref[...]
        vext[:, hd:] = jnp.ones((t, hd), BF16)
        col = lax.broadcasted_iota(jnp.int32, (MOBA_BLOCK, hd), 1)
        for j in range(t // MOBA_BLOCK):
            kaug[j * MOBA_BLOCK:(j + 1) * MOBA_BLOCK, hd:] = jnp.where(col == j, 1.0, 0.0).astype(BF16)

    q = q_ref[...]
    nbp = -(-(t // MOBA_BLOCK) // SUBLANES) * SUBLANES
    gate = _gate_scores_t(q, km_ref[:nbp, :])
    blk = lax.broadcasted_iota(jnp.int32, gate.shape, 0)
    own = i * (tq // MOBA_BLOCK) + lax.broadcasted_iota(jnp.int32, gate.shape, 1) // MOBA_BLOCK
    sel = _topk_mask(gate, blk < own, 0)
    bias = jnp.where(blk == own, 0.0, jnp.where(sel > 0.5, 0.0, MASKED))
    bias = jnp.concatenate([bias, jnp.zeros((LANES - nbp, tq), F32)], axis=0)
    qa[:, :hd] = q
    qa[:, hd:] = bias.T.astype(BF16)
    m_ref[...] = jnp.full(m_ref.shape, MASKED, F32)
    acc_ref[...] = jnp.zeros(acc_ref.shape, F32)

    n_lt = tq // LANES

    def lane_tile_max(s):
        pm = s[:, :LANES]
        for j in range(1, n_lt):
            pm = jnp.maximum(pm, s[:, j * LANES:(j + 1) * LANES])
        return pm

    def scores_to(slot, e):
        s_ref, mx_ref = slot
        st = pl.multiple_of(e * tq, tq)
        s = _dot_nt(qa[...], kaug[pl.ds(st, tq), :])
        s_ref[...] = s
        mx_ref[...] = lane_tile_max(s)

    def absorb(slot, e):
        s_ref, mx_ref = slot
        st = pl.multiple_of(e * tq, tq)
        m = m_ref[...]
        m_new = jnp.maximum(m, jnp.broadcast_to(jnp.max(mx_ref[...], axis=1, keepdims=True), m.shape))
        p = jnp.concatenate([jnp.exp2(s_ref[:, j * LANES:(j + 1) * LANES] - m_new).astype(BF16)
                             for j in range(n_lt)], axis=1)
        alpha = jnp.exp2(m - m_new)
        pv = _dot(p, vext[pl.ds(st, tq), :])
        acc_ref[:, :hd] = alpha * acc_ref[:, :hd] + pv[:, :hd]
        acc_ref[:, hd:] = alpha * acc_ref[:, hd:] + pv[:, hd:]
        m_ref[...] = m_new

    def absorb_own(slot):
        s_ref, mx_ref = slot
        r = lax.broadcasted_iota(jnp.int32, s_ref.shape, 0)
        c = lax.broadcasted_iota(jnp.int32, s_ref.shape, 1)
        s = jnp.where(c <= r, s_ref[...], MASKED)
        s_ref[...] = s
        mx_ref[...] = lane_tile_max(s)
        absorb(slot, i)
        o_ref[...] = (acc_ref[:, :hd] / acc_ref[:, hd:]).astype(o_ref.dtype)

    slot0, slot1 = (s0, mx0), (s1, mx1)
    scores_to(slot0, 0)

    def pair(p, carry):
        e = 2 * p
        scores_to(slot1, e + 1)
        absorb(slot0, e)
        scores_to(slot0, e + 2)
        absorb(slot1, e + 1)
        return carry

    lax.fori_loop(0, i // 2, pair, 0)

    @pl.when(i % 2 == 0)
    def _():
        absorb_own(slot0)

    @pl.when(i % 2 == 1)
    def _():
        scores_to(slot1, i)
        absorb(slot0, i - 1)
        absorb_own(slot1)


def _moba_prompt(q, k, v, kmean, *, blocks_per_tile=2):
    b, t, w = q.shape
    tq = blocks_per_tile * MOBA_BLOCK
    hd = HEAD_DIM_A
    assert t % tq == 0 and t // MOBA_BLOCK <= LANES and kmean.shape[1] == LANES
    return pl.pallas_call(
        _moba_prompt_kernel,
        out_shape=jax.ShapeDtypeStruct((b, t, w), BF16),
        grid=(b, w // hd, t // tq),
        in_specs=[pl.BlockSpec((None, tq, hd), lambda bb, hh, i: (bb, i, hh)),
                  pl.BlockSpec((None, t, hd), lambda bb, hh, i: (bb, 0, hh)),
                  pl.BlockSpec((None, t, hd), lambda bb, hh, i: (bb, 0, hh)),
                  pl.BlockSpec((None, LANES, hd), lambda bb, hh, i: (bb, 0, hh))],
        out_specs=pl.BlockSpec((None, tq, hd), lambda bb, hh, i: (bb, i, hh)),
        scratch_shapes=[pltpu.VMEM((t, 2 * hd), BF16), pltpu.VMEM((t, 2 * hd), BF16),
                        pltpu.VMEM((tq, 2 * hd), BF16),
                        pltpu.VMEM((tq, tq), F32), pltpu.VMEM((tq, tq), F32),
                        pltpu.VMEM((tq, LANES), F32), pltpu.VMEM((tq, LANES), F32),
                        pltpu.VMEM((tq, LANES), F32), pltpu.VMEM((tq, 2 * hd), F32)],
        compiler_params=_params(("parallel", "parallel", "arbitrary")),
        name="moba_prompt",
    )(q, k, v, kmean)


def _mlstm_kernel(qk_ref, v_ref, gc_ref, gr_ref, c0_ref, n0_ref, m0_ref,
                  h_ref, c_out, n_out, m_out, c_s, n_s, m_s, *, valid_rows):
    ci = pl.program_id(1)
    L = qk_ref.shape[0]
    wqk = N_HEADS_B * DK_B

    @pl.when(ci == 0)
    def _():
        c_s[...] = c0_ref[...]
        n_s[...] = n0_ref[...]
        m_s[...] = m0_ref[...]

    gc = gc_ref[...]
    gr = gr_ref[...]
    if valid_rows < L:
        rr = lax.broadcasted_iota(jnp.int32, gc.shape, 0)
        ln = lax.broadcasted_iota(jnp.int32, gc.shape, 1)
        gc = jnp.where(rr < valid_rows, gc, jnp.where(ln < N_HEADS_B, MASKED, 0.0))
        cc = lax.broadcasted_iota(jnp.int32, gr.shape, 1)
        rw = lax.broadcasted_iota(jnp.int32, gr.shape, 0)
        gr = jnp.where(cc < valid_rows, gr, jnp.where(rw < N_HEADS_B, MASKED, 0.0))

    r = lax.broadcasted_iota(jnp.int32, (L, L), 0)
    c = lax.broadcasted_iota(jnp.int32, (L, L), 1)
    causal = c <= r
    tri = jnp.where(causal, 1.0, 0.0).astype(BF16)
    triu = jnp.where(r <= c, 1.0, 0.0).astype(BF16)
    bc_all = sum(_dot(tri, p) for p in _split_bf16(gc, 3))
    br_all = sum(_dot(p, triu) for p in _split_bf16(gr, 3))

    for hh in range(N_HEADS_B):
        q = qk_ref[:, hh * DK_B:(hh + 1) * DK_B]
        k = qk_ref[:, wqk + hh * DK_B:wqk + (hh + 1) * DK_B]
        v = v_ref[:, hh * DV_B:(hh + 1) * DV_B]
        i_col = gc[:, hh:hh + 1]
        b_col = bc_all[:, N_HEADS_B + hh:N_HEADS_B + hh + 1]
        i_row = gr[hh:hh + 1, :]
        b_row = br_all[N_HEADS_B + hh:N_HEADS_B + hh + 1, :]
        C = c_s[hh]
        n = n_s[hh]
        m_prev = m_s[hh][:, :1]

        dmat = jnp.where(causal, b_col - b_row + i_row, MASKED)
        inter = b_col + m_prev
        m_t = jnp.maximum(inter, jnp.max(dmat, axis=1, keepdims=True))
        w = jnp.exp(dmat - m_t)
        a_inter = jnp.exp(inter - m_t)
        qk = _dot_nt(q, k) * w
        num = a_inter * _dot_nt(q, C.astype(BF16)) + _dot(qk.astype(BF16), v)
        qn = jnp.sum(q.astype(F32) * n, axis=1, keepdims=True)
        den = a_inter * qn + jnp.sum(qk, axis=1, keepdims=True)
        hout = num / jnp.maximum(jnp.abs(den), jnp.exp(-m_t))
        h_ref[:, hh * DV_B:(hh + 1) * DV_B] = hout.astype(h_ref.dtype)

        b_last = b_col[L - 1:L, :]
        g_col = b_last - b_col + i_col
        m_new = jnp.maximum(b_last + m_prev, jnp.max(g_col, axis=0, keepdims=True))
        decay = jnp.exp(b_last + m_prev - m_new)
        ws = jnp.exp(g_col - m_new)
        kw = k.astype(F32) * ws
        c_s[hh] = decay * C + _dot_tn(v, kw.astype(BF16))
        n_s[hh] = decay * n + jnp.sum(kw, axis=0, keepdims=True)
        m_s[hh] = jnp.broadcast_to(m_new, m_s.shape[1:])

    @pl.when(ci == pl.num_programs(1) - 1)
    def _():
        c_out[...] = c_s[...]
        n_out[...] = n_s[...]
        m_out[...] = m_s[...]


def _mlstm(qkb, vb, gates_col, gates_row, c0, n0, m0, *, chunk, valid_rows):
    b, t, _ = qkb.shape
    nc = t // chunk
    hb = N_HEADS_B
    st4 = lambda shp: pl.BlockSpec((None,) + shp, lambda bb, ci: (bb, 0, 0, 0))
    kern = functools.partial(_mlstm_kernel, valid_rows=valid_rows)
    return pl.pallas_call(
        kern,
        out_shape=(jax.ShapeDtypeStruct((b, t, hb * DV_B), BF16),
                   jax.ShapeDtypeStruct(c0.shape, F32),
                   jax.ShapeDtypeStruct(n0.shape, F32),
                   jax.ShapeDtypeStruct(m0.shape, F32)),
        grid=(b, nc),
        in_specs=[pl.BlockSpec((None, chunk, qkb.shape[2]), lambda bb, ci: (bb, ci, 0)),
                  pl.BlockSpec((None, chunk, vb.shape[2]), lambda bb, ci: (bb, ci, 0)),
                  pl.BlockSpec((None, chunk, LANES), lambda bb, ci: (bb, ci, 0)),
                  pl.BlockSpec((None, 2 * hb, chunk), lambda bb, ci: (bb * nc + ci, 0, 0)),
                  st4((hb, DV_B, DK_B)), st4((hb, 1, DK_B)), st4((hb, 1, LANES))],
        out_specs=(pl.BlockSpec((None, chunk, hb * DV_B), lambda bb, ci: (bb, ci, 0)),
                   st4((hb, DV_B, DK_B)), st4((hb, 1, DK_B)), st4((hb, 1, LANES))),
        scratch_shapes=[pltpu.VMEM((hb, DV_B, DK_B), F32),
                        pltpu.VMEM((hb, 1, DK_B), F32),
                        pltpu.VMEM((hb, 1, LANES), F32)],
        compiler_params=_params(("parallel", "arbitrary")),
        name="mlstm",
    )(qkb, vb, gates_col, gates_row, c0, n0, m0)


def _ffn_kernel(x_ref, ya_ref, hb_ref, sga_ref, sgob_ref, gtm_ref, shf_ref, scf_ref, gtf_ref,
                gffn_ref, gfin_ref, wout_ref, wup_ref, wconv_ref, bconv_ref, wdown_ref, *rest, seq_rows):
    if seq_rows:
        fix1_ref, fix2_ref, y_ref, a_ref = rest
    else:
        y_ref, tail_ref, carry = rest
    tm = x_ref.shape[0]
    dff = wdown_ref.shape[0]
    merged = (sga_ref[...].astype(F32) * ya_ref[...].astype(F32)
              + sgob_ref[...].astype(F32) * hb_ref[...].astype(F32))
    x1 = x_ref[...] + gtm_ref[...] * _dot(merged.astype(BF16), wout_ref[...])
    ms = jnp.mean(x1 * x1, axis=-1, keepdims=True)
    xn = x1 * lax.rsqrt(ms + RMS_EPS) * gffn_ref[...]
    h = (xn * (1.0 + scf_ref[...]) + shf_ref[...]).astype(BF16)
    a = _dot(h, wup_ref[:, :dff])
    gb = _dot(h, wup_ref[:, dff:])

    row = lax.broadcasted_iota(jnp.int32, a.shape, 0)
    r1 = pltpu.roll(a, 1, axis=0)
    r2 = pltpu.roll(a, 2, axis=0)
    if seq_rows:
        assert seq_rows & (seq_rows - 1) == 0
        t = row & (seq_rows - 1)
        a1 = jnp.where(t >= 1, r1, fix1_ref[...])
        a2 = jnp.where(t >= 2, r2, fix2_ref[...])
        a_ref[...] = a
    else:
        @pl.when(pl.program_id(1) == 0)
        def _():
            carry[...] = jnp.zeros_like(carry)
        prev1 = carry[SUBLANES - 1:SUBLANES, :]
        prev2 = carry[SUBLANES - 2:SUBLANES - 1, :]
        a1 = jnp.where(row == 0, prev1, r1)
        a2 = jnp.where(row == 0, prev2, jnp.where(row == 1, prev1, r2))
        tail = a[tm - SUBLANES:, :]
        carry[...] = tail
        tail_ref[...] = tail
    ac = bconv_ref[...] + a2 * wconv_ref[0:1, :] + a1 * wconv_ref[1:2, :] + a * wconv_ref[2:3, :]
    gelu = 0.5 * ac * (1.0 + jnp.tanh(math.sqrt(2.0 / math.pi) * (ac + 0.044715 * (ac * ac * ac))))
    f = _dot((gelu * gb).astype(BF16), wdown_ref[...])
    x2 = x1 + gtf_ref[...] * f
    ms2 = jnp.mean(x2 * x2, axis=-1, keepdims=True)
    y_ref[...] = x2 * lax.rsqrt(ms2 + RMS_EPS) * gfin_ref[...]


def _ffn(x, ya, hb, sga, sgob, gtm, shf, scf, gtf, g_ffn, g_fin, w_out, w_up, w_conv, b_conv, w_down,
         *, batch, seq_rows=0, fix1=None, fix2=None):
    n, d = x.shape
    dff = w_down.shape[0]
    tm = ROW_TILE
    if seq_rows:
        assert n == tm
        grid = (1, 1)
        mod_spec = pl.BlockSpec((tm, d), lambda bb, i: (0, 0))
        nt = 1
    else:
        nt = n // batch // tm
        grid = (batch, nt)
        mod_spec = pl.BlockSpec((None, 1, d), lambda bb, i: (bb, 0, 0))
    row = lambda w: pl.BlockSpec((tm, w), lambda bb, i: (bb * nt + i, 0))
    in_specs = [row(d), row(d), row(d), row(d), row(d), mod_spec, mod_spec, mod_spec, mod_spec,
                _resident((1, d)), _resident((1, d)), _resident(w_out.shape), _resident(w_up.shape),
                _resident(w_conv.shape), _resident((1, dff)), _resident(w_down.shape)]
    args = [x, ya, hb, sga, sgob, gtm, shf, scf, gtf, g_ffn, g_fin, w_out, w_up, w_conv, b_conv, w_down]
    if seq_rows:
        in_specs += [row(dff), row(dff)]
        args += [fix1, fix2]
        out_shape = (jax.ShapeDtypeStruct((n, d), F32), jax.ShapeDtypeStruct((n, dff), F32))
        out_specs = (row(d), row(dff))
        scratch = []
    else:
        out_shape = (jax.ShapeDtypeStruct((n, d), F32), jax.ShapeDtypeStruct((batch, SUBLANES, dff), F32))
        out_specs = (row(d), pl.BlockSpec((None, SUBLANES, dff), lambda bb, i: (bb, 0, 0)))
        scratch = [pltpu.VMEM((SUBLANES, dff), F32)]
    return pl.pallas_call(
        functools.partial(_ffn_kernel, seq_rows=seq_rows),
        out_shape=out_shape,
        grid=grid,
        in_specs=in_specs,
        out_specs=out_specs,
        scratch_shapes=scratch,
        compiler_params=_params(("parallel", "arbitrary")),
        name="merge_ffn",
    )(*args)


def _kmean_kernel(pt_ref, *refs):
    pages = refs[:-1]
    o_ref = refs[-1]
    per_blk = MOBA_BLOCK // pages[0].shape[0]
    for r in range(len(pages) // per_blk):
        s = jnp.sum(pages[r * per_blk][...], axis=0)
        for p in range(1, per_blk):
            s = s + jnp.sum(pages[r * per_blk + p][...], axis=0)
        o_ref[r] = s * (1.0 / MOBA_BLOCK)


def _cache_kmean(cache_k, page_table, *, pages_per_step=8):
    _, page, nh, hd = cache_k.shape
    b, n_pages = page_table.shape
    assert MOBA_BLOCK % page == 0 and n_pages % pages_per_step == 0
    per_blk = MOBA_BLOCK // page
    blk_per_step = pages_per_step // per_blk
    steps = n_pages // pages_per_step
    in_specs = [pl.BlockSpec((None, page, nh, hd), functools.partial(
        lambda bb, g, pt, p: (pt[bb, g * pages_per_step + p], 0, 0, 0), p=p)) for p in range(pages_per_step)]
    return pl.pallas_call(
        _kmean_kernel,
        out_shape=jax.ShapeDtypeStruct((b, n_pages // per_blk, nh, hd), F32),
        grid_spec=pltpu.PrefetchScalarGridSpec(
            num_scalar_prefetch=1,
            grid=(b, steps),
            in_specs=in_specs,
            out_specs=pl.BlockSpec((None, blk_per_step, nh, hd), lambda bb, g, pt: (bb, g, 0, 0))),
        compiler_params=_params(("parallel", "arbitrary")),
        name="cache_kmean",
    )(page_table, *([cache_k] * pages_per_step))


def _select_kernel(q_ref, km_ref, o_ref):
    n_blocks = km_ref.shape[0]
    ts = q_ref.shape[0]
    lane = lax.broadcasted_iota(jnp.int32, (ts, LANES), 1)
    for hh in range(N_HEADS_A):
        sl = slice(hh * HEAD_DIM_A, (hh + 1) * HEAD_DIM_A)
        gate = _gate_scores(q_ref[:, sl].astype(BF16), km_ref[:, hh, :])
        g = gate
        blk = lax.broadcasted_iota(jnp.int32, gate.shape, 1).astype(F32)
        out = jnp.zeros((ts, LANES), F32)
        for r in range(MOBA_TOPK):
            mx = jnp.max(g, axis=1, keepdims=True)
            idx = jnp.min(jnp.where(g == mx, blk, float(n_blocks)), axis=1, keepdims=True)
            out = jnp.where(lane == r, idx, out)
            g = jnp.where(blk == idx, -jnp.inf, g)
        o_ref[hh] = out.astype(jnp.int32)


def _select_blocks(q, kmean):
    b, ts, w = q.shape
    _, nb, nh, hd = kmean.shape
    return pl.pallas_call(
        _select_kernel,
        out_shape=jax.ShapeDtypeStruct((b, N_HEADS_A, ts, LANES), jnp.int32),
        grid=(b,),
        in_specs=[pl.BlockSpec((None, ts, w), lambda bb: (bb, 0, 0)),
                  pl.BlockSpec((None, nb, nh, hd), lambda bb: (bb, 0, 0, 0))],
        out_specs=pl.BlockSpec((None, N_HEADS_A, ts, LANES), lambda bb: (bb, 0, 0, 0)),
        compiler_params=_params(("parallel",)),
        name="select_blocks",
    )(q, kmean)


def _moba_sample_kernel(sel_ref, pt_ref, q_ref, kn_ref, vn_ref, ck_ref, cv_ref, o_ref, kbuf, vbuf, sem,
                        *, n_pages, per_blk):
    b = pl.program_id(0)
    hh = pl.program_id(1)
    ts = q_ref.shape[0]
    page = kbuf.shape[1]
    nsel = MOBA_TOPK * per_blk

    step = b * N_HEADS_A + hh
    nsteps = pl.num_programs(0) * N_HEADS_A
    nslots = ts * nsel

    def copies(stp, t, r, p):
        blk = sel_ref[(stp * ts + t) * MOBA_TOPK + r]
        pg = pt_ref[(stp // N_HEADS_A) * n_pages + blk * per_blk + p]
        head = stp % N_HEADS_A
        half = stp % 2
        slot = half * nslots + (t * MOBA_TOPK + r) * per_blk + p
        return (pltpu.make_async_copy(ck_ref.at[pg, :, head, :], kbuf.at[slot], sem.at[0, half]),
                pltpu.make_async_copy(cv_ref.at[pg, :, head, :], vbuf.at[slot], sem.at[1, half]))

    def for_all(stp, fn):
        for t in range(ts):
            for r in range(MOBA_TOPK):
                for p in range(per_blk):
                    ck, cv = copies(stp, t, r, p)
                    fn(ck)
                    fn(cv)

    @pl.when(step == 0)
    def _():
        for_all(step, lambda c: c.start())

    @pl.when(step + 1 < nsteps)
    def _():
        for_all(step + 1, lambda c: c.start())

    for_all(step, lambda c: c.wait())

    base = (step % 2) * nslots
    qf = q_ref[...]
    q = qf.astype(BF16)
    kn = kn_ref[...]
    vn = vn_ref[...]
    jrow = lax.broadcasted_iota(jnp.int32, (ts, LANES), 0)
    for t in range(ts):
        qrep = jnp.broadcast_to(q[t:t + 1, :], (LANES, q.shape[1]))
        kc = kbuf[pl.ds(base + t * nsel, nsel)].reshape(nsel * page, -1).astype(BF16)
        vc = vbuf[pl.ds(base + t * nsel, nsel)].reshape(nsel * page, -1)
        s = _dot_nt(kc, qrep)
        s_own = jnp.where(jrow <= t, jnp.sum(kn * qf[t:t + 1, :], axis=1, keepdims=True), MASKED)
        m = jnp.maximum(jnp.max(s, axis=0, keepdims=True), jnp.max(s_own, axis=0, keepdims=True))
        p = jnp.exp2(s - m)
        p_own = jnp.exp2(s_own - m)
        l = jnp.sum(p, axis=0, keepdims=True) + jnp.sum(p_own, axis=0, keepdims=True)
        o = jnp.sum(p * vc, axis=0, keepdims=True) + jnp.sum(p_own * vn, axis=0, keepdims=True)
        o_ref[t:t + 1, :] = o / l


def _moba_sample(q, k_new, v_new, cache_k, cache_v, page_table, sel):
    b, n_pages = page_table.shape
    ts = q.shape[0] // b
    _, page, nh, hd = cache_k.shape
    per_blk = MOBA_BLOCK // page
    nslots = ts * MOBA_TOPK * per_blk
    blk = pl.BlockSpec((ts, hd), lambda bb, hh, *_: (bb, hh))
    kern = functools.partial(_moba_sample_kernel, n_pages=n_pages, per_blk=per_blk)
    return pl.pallas_call(
        kern,
        out_shape=jax.ShapeDtypeStruct(q.shape, F32),
        grid_spec=pltpu.PrefetchScalarGridSpec(
            num_scalar_prefetch=2,
            grid=(b, nh),
            in_specs=[blk, blk, blk, pl.BlockSpec(memory_space=pl.ANY), pl.BlockSpec(memory_space=pl.ANY)],
            out_specs=blk,
            scratch_shapes=[pltpu.VMEM((2 * nslots, page, hd), F32), pltpu.VMEM((2 * nslots, page, hd), F32),
                            pltpu.SemaphoreType.DMA((2, 2))]),
        compiler_params=_params(("arbitrary", "arbitrary")),
        name="moba_sample",
    )(sel.reshape(-1), page_table.reshape(-1), q, k_new, v_new, cache_k, cache_v)


def _rope_tables(pos):
    half = HEAD_DIM_A // 2
    inv = ROPE_THETA ** (-jnp.arange(half, dtype=F32) / half)
    ang = pos.astype(F32)[:, None] * inv[None, :]
    cos, sin = jnp.cos(ang), jnp.sin(ang)
    return jnp.concatenate([cos, cos], axis=1), jnp.concatenate([-sin, sin], axis=1)


def _pack_w_in(w_in):
    d = w_in.shape[0]
    a = 3 * N_HEADS_A * HEAD_DIM_A + 2 * N_HEADS_B * DK_B + 2 * N_HEADS_B * DV_B
    ng = 2 * N_HEADS_B
    w_main = jnp.concatenate([w_in[:, :a], w_in[:, a + ng:]], axis=1).astype(BF16)
    w_gate = jnp.zeros((d, LANES), BF16).at[:, :ng].set(w_in[:, a:a + ng].astype(BF16))
    return w_main, w_gate


def _layer(x, c_mod, pos, w, *, sample=None):
    (g_mix, w_main, w_gate, b_gate, w_out, g_ffn, w_up, w_conv, b_conv, w_down, g_fin) = w
    b, t, d = x.shape
    n = b * t
    wa = N_HEADS_A * HEAD_DIM_A
    dff = w_down.shape[0]
    hb = N_HEADS_B
    sh_m, sc_m, gt_m, sh_f, sc_f, gt_f = jnp.split(c_mod, 6, axis=-1)
    cos2, sin2 = _rope_tables(pos)
    x2d = x.reshape(n, d)

    if sample is None:
        per_seq = lambda a: a.reshape(b, 1, d)
        q, kf, kb16, vf, vb16, qkb, vbb, sga, sgob, gcol, grow, kmean = _inproj(
            x2d, g_mix, per_seq(sh_m), per_seq(sc_m), cos2, sin2, w_main, w_gate, b_gate,
            rows_per_mod=t, rows_per_pos=t)
        kmean = jnp.pad(kmean.reshape(b, t // MOBA_BLOCK, wa), ((0, 0), (0, LANES - t // MOBA_BLOCK), (0, 0)))
        ya = _moba_prompt(q.reshape(b, t, wa), kb16.reshape(b, t, wa), vb16.reshape(b, t, wa),
                          kmean).reshape(n, wa)
        chunk = MLSTM_CHUNK
        nc = t // chunk
        grow3 = grow.reshape(2 * hb, b * nc, chunk).transpose(1, 0, 2)
        c0 = jnp.zeros((b, hb, DV_B, DK_B), F32)
        n0 = jnp.zeros((b, hb, 1, DK_B), F32)
        m0 = jnp.zeros((b, hb, 1, LANES), F32)
        hbo, c_new, n_new, m_new = _mlstm(qkb.reshape(b, t, -1), vbb.reshape(b, t, -1), gcol.reshape(b, t, LANES),
                                          grow3, c0, n0, m0, chunk=chunk, valid_rows=chunk)
        y, tail = _ffn(x2d, ya, hbo.reshape(n, -1), sga, sgob, per_seq(gt_m), per_seq(sh_f), per_seq(sc_f),
                       per_seq(gt_f), g_ffn, g_fin, w_out, w_up, w_conv, b_conv, w_down, batch=b)
        conv_new = tail[:, SUBLANES - (CONV_W - 1):, :]
    else:
        cache_k, cache_v, page_table, c_st, n_st, m_st, conv0 = sample
        assert n == ROW_TILE and t >= CONV_W - 1
        past = page_table.shape[1] * cache_k.shape[1]
        assert past % MOBA_BLOCK == 0 and t <= MOBA_BLOCK
        per_row = lambda a: jnp.repeat(a, t, axis=0)
        cos2, sin2 = jnp.tile(cos2, (b, 1)), jnp.tile(sin2, (b, 1))
        q, kf, kb16, vf, vb16, qkb, vbb, sga, sgob, gcol, grow, _ = _inproj(
            x2d, g_mix, per_row(sh_m), per_row(sc_m), cos2, sin2, w_main, w_gate, b_gate,
            rows_per_mod=0, rows_per_pos=n)
        kmean = _cache_kmean(cache_k, page_table, pages_per_step=min(16, page_table.shape[1]))
        qf = q.astype(F32)
        sel = _select_blocks(qf.reshape(b, t, wa), kmean)[..., :MOBA_TOPK]
        ya = _moba_sample(qf, kf, vf, cache_k, cache_v, page_table, sel).astype(BF16)
        chunk = SAMPLE_CHUNK
        pad = lambda a: jnp.pad(a.reshape(b, t, -1), ((0, 0), (0, chunk - t), (0, 0)))
        grow3 = jnp.pad(grow.reshape(2 * hb, b, t).transpose(1, 0, 2), ((0, 0), (0, 0), (0, chunk - t)))
        m0 = jnp.broadcast_to(m_st[:, :, None, None], (b, hb, 1, LANES))
        hbo, c_new, n_new, m_new = _mlstm(pad(qkb), pad(vbb), pad(gcol), grow3, c_st, n_st[:, :, None, :], m0,
                                          chunk=chunk, valid_rows=t)
        hbo = hbo[:, :t].reshape(n, -1)
        zero = jnp.zeros((b, t - 2, dff), F32)
        fix1 = jnp.concatenate([conv0[:, 1:2], zero, zero[:, :1]], axis=1).reshape(n, dff)
        fix2 = jnp.concatenate([conv0, zero], axis=1).reshape(n, dff)
        y, a_full = _ffn(x2d, ya, hbo, sga, sgob, per_row(gt_m), per_row(sh_f), per_row(sc_f), per_row(gt_f),
                         g_ffn, g_fin, w_out, w_up, w_conv, b_conv, w_down, batch=b, seq_rows=t,
                         fix1=fix1, fix2=fix2)
        conv_new = a_full.reshape(b, t, dff)[:, t - (CONV_W - 1):]
    return (y.reshape(b, t, d), kf.reshape(b, t, N_HEADS_A, HEAD_DIM_A), vf.reshape(b, t, N_HEADS_A, HEAD_DIM_A),
            c_new, n_new[:, :, 0, :], m_new[:, :, 0, 0], conv_new)


def kernel(x_prompt, x_sample, cache_k, cache_v, page_table, state_C, state_n, state_m, state_conv,
           c_prompt, c_sample, w_ada, b_ada, g_norm_mix, w_in, b_gates, w_out, g_norm_ffn,
           w_up, w_conv, b_conv, w_down, g_norm_final):
    depth = w_in.shape[0]
    assert depth == 1
    bp, tp, d = x_prompt.shape
    bs, ts, _ = x_sample.shape
    past = page_table.shape[1] * cache_k.shape[2]
    dl = 0
    mod = _ada(jnp.concatenate([c_prompt, c_sample], axis=0), w_ada[dl], b_ada[dl])
    w_main, w_gate = _pack_w_in(w_in[dl])
    b_gate = jnp.zeros((1, LANES), F32).at[0, :2 * N_HEADS_B].set(b_gates[dl])
    w = (g_norm_mix[dl].reshape(1, d), w_main, w_gate, b_gate, w_out[dl].astype(BF16),
         g_norm_ffn[dl].reshape(1, d), w_up[dl].astype(BF16), w_conv[dl], b_conv[dl].reshape(1, -1),
         w_down[dl].astype(BF16), g_norm_final.reshape(1, d))
    yp, kp, vp, cp, np_, mp, cvp = _layer(x_prompt, mod[:bp], jnp.arange(tp, dtype=jnp.int32), w)
    ys, ks, vs, cs, ns, ms, cvs = _layer(
        x_sample, mod[bp:], past + jnp.arange(ts, dtype=jnp.int32), w,
        sample=(cache_k[dl], cache_v[dl], page_table, state_C[dl], state_n[dl], state_m[dl], state_conv[dl]))
    st = lambda a: a[None]
    return (yp, ys, st(kp), st(vp), st(ks), st(vs), st(cp), st(np_), st(mp), st(cs), st(ns), st(ms),
            st(cvp), st(cvs))
```

```python
import functools
import math

import jax
import jax.numpy as jnp
from jax import lax
from jax.experimental import pallas as pl
from jax.experimental.pallas import tpu as pltpu

F32 = jnp.float32
BF16 = jnp.bfloat16

N_HEADS_A = 8
HEAD_DIM_A = 128
MOBA_BLOCK = 256
MOBA_TOPK = 3
ROPE_THETA = 10000.0
N_HEADS_B = 4
DK_B = 128
DV_B = 256
CONV_W = 3
RMS_EPS = 1e-6
MASKED = -1e30
LANES = 128
SUBLANES = 8
ROW_TILE = 256
MLSTM_CHUNK = 256
SAMPLE_CHUNK = 128
VMEM_LIMIT = 56 * 1024 * 1024


def _resident(shape):
    nd = len(shape)
    return pl.BlockSpec(shape, lambda *_: (0,) * nd, pipeline_mode=pl.Buffered(1))


def _params(sem):
    return pltpu.CompilerParams(dimension_semantics=sem, vmem_limit_bytes=VMEM_LIMIT)


def _split_bf16(x, pieces):
    out = []
    r = x
    for _ in range(pieces):
        p = r.astype(BF16)
        out.append(p)
        r = r - p.astype(F32)
    return out


def _dot_nt(a, b):
    return lax.dot_general(a, b, (((1,), (1,)), ((), ())), preferred_element_type=F32)


def _dot_tn(a, b):
    return lax.dot_general(a, b, (((0,), (0,)), ((), ())), preferred_element_type=F32)


def _dot(a, b):
    return jnp.dot(a, b, preferred_element_type=F32)


def _ada_kernel(c_ref, w_ref, b_ref, o_ref):
    c = c_ref[...]
    s = (c * jax.nn.sigmoid(c)).astype(BF16)
    o_ref[...] = _dot(s, w_ref[...].astype(BF16)) + b_ref[...]


def _ada(c, w_ada, b_ada):
    n, d = c.shape
    d_out = w_ada.shape[1]
    tn = d
    return pl.pallas_call(
        _ada_kernel,
        out_shape=jax.ShapeDtypeStruct((n, d_out), F32),
        grid=(d_out // tn,),
        in_specs=[pl.BlockSpec((n, d), lambda j: (0, 0)),
                  pl.BlockSpec((d, tn), lambda j: (0, j)),
                  pl.BlockSpec((1, tn), lambda j: (0, j))],
        out_specs=pl.BlockSpec((n, tn), lambda j: (0, j)),
        compiler_params=_params(("arbitrary",)),
        name="ada_mod",
    )(c, w_ada, b_ada.reshape(1, d_out))


def _inproj_kernel(x_ref, g_ref, sh_ref, sc_ref, cos_ref, sin_ref, w_ref, wg_ref, bg_ref,
                   q_ref, kf_ref, kb_ref, vf_ref, vb_ref, qkb_ref, vbb_ref, sga_ref, sgob_ref,
                   gc_ref, gr_ref, km_ref):
    tm, d = x_ref.shape
    wa = N_HEADS_A * HEAD_DIM_A
    x = x_ref[...]
    ms = jnp.mean(x * x, axis=-1, keepdims=True)
    xn = x * lax.rsqrt(ms + RMS_EPS) * g_ref[...]
    h = (xn * (1.0 + sc_ref[...]) + sh_ref[...]).astype(BF16)
    cos = cos_ref[...]
    sin = sin_ref[...]

    def proj(c):
        return _dot(h, w_ref[:, c * wa:(c + 1) * wa])

    def rope(a, hh):
        xh = a[:, hh * HEAD_DIM_A:(hh + 1) * HEAD_DIM_A]
        return xh * cos + pltpu.roll(xh, HEAD_DIM_A // 2, axis=1) * sin

    qscale = HEAD_DIM_A ** -0.5 * math.log2(math.e)
    a = proj(0)
    for hh in range(N_HEADS_A):
        sl = slice(hh * HEAD_DIM_A, (hh + 1) * HEAD_DIM_A)
        q_ref[:, sl] = (rope(a, hh) * qscale).astype(BF16)
    a = proj(1)
    for hh in range(N_HEADS_A):
        sl = slice(hh * HEAD_DIM_A, (hh + 1) * HEAD_DIM_A)
        kr = rope(a, hh)
        kf_ref[:, sl] = kr
        kb_ref[:, sl] = kr.astype(BF16)
        for g in range(tm // MOBA_BLOCK):
            km_ref[g, :, sl] = jnp.mean(kr[g * MOBA_BLOCK:(g + 1) * MOBA_BLOCK], axis=0, keepdims=True)
    a = proj(2)
    vf_ref[...] = a
    vb_ref[...] = a.astype(BF16)
    a = proj(3)
    wqk = N_HEADS_B * DK_B
    qkb_ref[:, :wqk] = a[:, :wqk].astype(BF16)
    qkb_ref[:, wqk:] = (a[:, wqk:] * (DK_B ** -0.5)).astype(BF16)
    vbb_ref[...] = proj(4).astype(BF16)
    so = jax.nn.sigmoid(proj(5))
    sga_ref[...] = jax.nn.sigmoid(proj(6)).astype(BF16)
    sgob_ref[...] = (jax.nn.sigmoid(proj(7)) * so).astype(BF16)
    g = _dot(h, wg_ref[...]) + bg_ref[...]
    lane = lax.broadcasted_iota(jnp.int32, g.shape, 1)
    logsig = jnp.minimum(g, 0.0) - jnp.log(1.0 + jnp.exp(-jnp.abs(g)))
    g = jnp.where(lane >= N_HEADS_B, logsig, g)
    gc_ref[...] = g
    gr_ref[...] = g.T[:2 * N_HEADS_B, :]


def _inproj(x, g_mix, shift, scale, cos2, sin2, w_main, w_gate, b_gate, *, rows_per_mod, rows_per_pos):
    n, d = x.shape
    tm = ROW_TILE
    assert n % tm == 0 and tm % MOBA_BLOCK == 0
    wa = N_HEADS_A * HEAD_DIM_A
    if rows_per_mod:
        assert rows_per_mod % tm == 0
        mod_spec = pl.BlockSpec((None, 1, d), lambda i: (i // (rows_per_mod // tm), 0, 0))
    else:
        mod_spec = pl.BlockSpec((tm, d), lambda i: (i, 0))
    assert rows_per_pos % tm == 0
    pos_spec = pl.BlockSpec((tm, HEAD_DIM_A), lambda i: (i % (rows_per_pos // tm), 0))
    row = lambda w: pl.BlockSpec((tm, w), lambda i: (i, 0))
    nblk = tm // MOBA_BLOCK
    out_shape = (
        jax.ShapeDtypeStruct((n, wa), BF16),
        jax.ShapeDtypeStruct((n, wa), F32),
        jax.ShapeDtypeStruct((n, wa), BF16),
        jax.ShapeDtypeStruct((n, wa), F32),
        jax.ShapeDtypeStruct((n, wa), BF16),
        jax.ShapeDtypeStruct((n, 2 * N_HEADS_B * DK_B), BF16),
        jax.ShapeDtypeStruct((n, N_HEADS_B * DV_B), BF16),
        jax.ShapeDtypeStruct((n, wa), BF16),
        jax.ShapeDtypeStruct((n, N_HEADS_B * DV_B), BF16),
        jax.ShapeDtypeStruct((n, LANES), F32),
        jax.ShapeDtypeStruct((2 * N_HEADS_B, n), F32),
        jax.ShapeDtypeStruct((n // MOBA_BLOCK, 1, wa), F32),
    )
    out_specs = (row(wa), row(wa), row(wa), row(wa), row(wa), row(2 * N_HEADS_B * DK_B), row(N_HEADS_B * DV_B),
                 row(wa), row(N_HEADS_B * DV_B), row(LANES),
                 pl.BlockSpec((2 * N_HEADS_B, tm), lambda i: (0, i)),
                 pl.BlockSpec((nblk, 1, wa), lambda i: (i, 0, 0)))
    return pl.pallas_call(
        _inproj_kernel,
        out_shape=out_shape,
        grid=(n // tm,),
        in_specs=[row(d), _resident((1, d)), mod_spec, mod_spec, pos_spec, pos_spec,
                  _resident(w_main.shape), _resident(w_gate.shape), _resident((1, LANES))],
        out_specs=out_specs,
        compiler_params=_params(("parallel",)),
        name="in_proj",
    )(x, g_mix, shift, scale, cos2, sin2, w_main, w_gate, b_gate)


def _topk_mask(gate, valid, axis):
    blk = lax.broadcasted_iota(jnp.int32, gate.shape, axis).astype(F32)
    g = jnp.where(valid, gate, -jnp.inf)
    sel = jnp.zeros(gate.shape, F32)
    for _ in range(MOBA_TOPK):
        mx = jnp.max(g, axis=axis, keepdims=True)
        idx = jnp.min(jnp.where(g == mx, blk, float(gate.shape[axis])), axis=axis, keepdims=True)
        pick = blk == idx
        sel = jnp.where(pick, jnp.where(valid, 1.0, 0.0), sel)
        g = jnp.where(pick, -jnp.inf, g)
    return sel


def _gate_scores(q, kmean):
    hi, lo = _split_bf16(kmean, 2)
    return _dot_nt(q, hi) + _dot_nt(q, lo)


def _gate_scores_t(q, kmean):
    hi, lo = _split_bf16(kmean, 2)
    return _dot_nt(hi, q) + _dot_nt(lo, q)


def _moba_prompt_kernel(pt_ref, q_ref, k_ref, v_ref, km_ref, ck_ref, o_ref, cm_ref,
                        kaug, vext, qa, s0, s1, mx0, mx1, m_ref, acc_ref, pbuf, psem, *, pages_per_step):
    i = pl.program_id(2)
    tq, hd = q_ref.shape
    t = k_ref.shape[0]
    step = (pl.program_id(0) * pl.num_programs(1) + pl.program_id(1)) * pl.num_programs(2) + i
    n_steps = pl.num_programs(0) * pl.num_programs(1) * pl.num_programs(2)

    def page_copy(stp, p):
        half = stp % 2
        return pltpu.make_async_copy(ck_ref.at[pt_ref[stp * pages_per_step + p]],
                                     pbuf.at[half * pages_per_step + p], psem.at[half])

    @pl.when(step == 0)
    def _():
        for p in range(pages_per_step):
            page_copy(step, p).start()

    @pl.when(step + 1 < n_steps)
    def _():
        for p in range(pages_per_step):
            page_copy(step + 1, p).start()

    for p in range(pages_per_step):
        page_copy(step, p).wait()
    per_blk = MOBA_BLOCK // pbuf.shape[1]
    for r in range(pages_per_step // per_blk):
        base = (step % 2) * pages_per_step + r * per_blk
        tot = jnp.sum(pbuf[base], axis=0)
        for p in range(1, per_blk):
            tot = tot + jnp.sum(pbuf[base + p], axis=0)
        cm_ref[r] = tot * (1.0 / MOBA_BLOCK)

    @pl.when(i == 0)
    def _():
        kaug[:, :hd] = k_ref[...]
        vext[:, :hd] = v_ref[...]
        vext[:, hd:] = jnp.ones((t, hd), BF16)
        col = lax.broadcasted_iota(jnp.int32, (MOBA_BLOCK, hd), 1)
        for j in range(t // MOBA_BLOCK):
            kaug[j * MOBA_BLOCK:(j + 1) * MOBA_BLOCK, hd:] = jnp.where(col == j, 1.0, 0.0).astype(BF16)

    q = q_ref[...]
    nbp = -(-(t // MOBA_BLOCK) // SUBLANES) * SUBLANES
    gate = _gate_scores_t(q, km_ref[:nbp, :])
    blk = lax.broadcasted_iota(jnp.int32, gate.shape, 0)
    own = i * (tq // MOBA_BLOCK) + lax.broadcasted_iota(jnp.int32, gate.shape, 1) // MOBA_BLOCK
    sel = _topk_mask(gate, blk < own, 0)
    bias = jnp.where(blk == own, 0.0, jnp.where(sel > 0.5, 0.0, MASKED))
    bias = jnp.concatenate([bias, jnp.zeros((LANES - nbp, tq), F32)], axis=0)
    qa[:, :hd] = q
    qa[:, hd:] = bias.T.astype(BF16)
    m_ref[...] = jnp.full(m_ref.shape, MASKED, F32)
    acc_ref[...] = jnp.zeros(acc_ref.shape, F32)

    n_lt = tq // LANES

    def lane_tile_max(s):
        pm = s[:, :LANES]
        for j in range(1, n_lt):
            pm = jnp.maximum(pm, s[:, j * LANES:(j + 1) * LANES])
        return pm

    def scores_to(slot, e):
        s_ref, mx_ref = slot
        st = pl.multiple_of(e * tq, tq)
        s = _dot_nt(qa[...], kaug[pl.ds(st, tq), :])
        s_ref[...] = s
        mx_ref[...] = lane_tile_max(s)

    def absorb(slot, e):
        s_ref, mx_ref = slot
        st = pl.multiple_of(e * tq, tq)
        m = m_ref[...]
        m_new = jnp.maximum(m, jnp.broadcast_to(jnp.max(mx_ref[...], axis=1, keepdims=True), m.shape))
        p = jnp.concatenate([jnp.exp2(s_ref[:, j * LANES:(j + 1) * LANES] - m_new).astype(BF16)
                             for j in range(n_lt)], axis=1)
        alpha = jnp.exp2(m - m_new)
        pv = _dot(p, vext[pl.ds(st, tq), :])
        acc_ref[:, :hd] = alpha * acc_ref[:, :hd] + pv[:, :hd]
        acc_ref[:, hd:] = alpha * acc_ref[:, hd:] + pv[:, hd:]
        m_ref[...] = m_new

    def absorb_own(slot):
        s_ref, mx_ref = slot
        r = lax.broadcasted_iota(jnp.int32, s_ref.shape, 0)
        c = lax.broadcasted_iota(jnp.int32, s_ref.shape, 1)
        s = jnp.where(c <= r, s_ref[...], MASKED)
        s_ref[...] = s
        mx_ref[...] = lane_tile_max(s)
        absorb(slot, i)
        o_ref[...] = (acc_ref[:, :hd] / acc_ref[:, hd:]).astype(o_ref.dtype)

    slot0, slot1 = (s0, mx0), (s1, mx1)
    scores_to(slot0, 0)

    def pair(p, carry):
        e = 2 * p
        scores_to(slot1, e + 1)
        absorb(slot0, e)
        scores_to(slot0, e + 2)
        absorb(slot1, e + 1)
        return carry

    lax.fori_loop(0, i // 2, pair, 0)

    @pl.when(i % 2 == 0)
    def _():
        absorb_own(slot0)

    @pl.when(i % 2 == 1)
    def _():
        scores_to(slot1, i)
        absorb(slot0, i - 1)
        absorb_own(slot1)


def _moba_prompt(q, k, v, kmean, cache_k, page_table, *, blocks_per_tile=2):
    b, t, w = q.shape
    tq = blocks_per_tile * MOBA_BLOCK
    hd = HEAD_DIM_A
    nh = w // hd
    assert t % tq == 0 and t // MOBA_BLOCK <= LANES and kmean.shape[1] == LANES
    n_steps = b * nh * (t // tq)
    bs, n_pages = page_table.shape
    page = cache_k.shape[1]
    per_blk = MOBA_BLOCK // page
    pages_per_step = bs * n_pages // n_steps
    assert pages_per_step * n_steps == bs * n_pages and pages_per_step % per_blk == 0 and MOBA_BLOCK % page == 0
    blk_per_step = pages_per_step // per_blk

    def step_of(bb, hh, i):
        return (bb * nh + hh) * (t // tq) + i

    ya, cmean = pl.pallas_call(
        functools.partial(_moba_prompt_kernel, pages_per_step=pages_per_step),
        out_shape=(jax.ShapeDtypeStruct((b, t, w), BF16),
                   jax.ShapeDtypeStruct((n_steps, blk_per_step) + cache_k.shape[2:], F32)),
        grid_spec=pltpu.PrefetchScalarGridSpec(
            num_scalar_prefetch=1,
            grid=(b, nh, t // tq),
            in_specs=[pl.BlockSpec((None, tq, hd), lambda bb, hh, i, pt: (bb, i, hh)),
                      pl.BlockSpec((None, t, hd), lambda bb, hh, i, pt: (bb, 0, hh)),
                      pl.BlockSpec((None, t, hd), lambda bb, hh, i, pt: (bb, 0, hh)),
                      pl.BlockSpec((None, LANES, hd), lambda bb, hh, i, pt: (bb, 0, hh)),
                      pl.BlockSpec(memory_space=pl.ANY)],
            out_specs=(pl.BlockSpec((None, tq, hd), lambda bb, hh, i, pt: (bb, i, hh)),
                       pl.BlockSpec((None, blk_per_step) + cache_k.shape[2:],
                                    lambda bb, hh, i, pt: (step_of(bb, hh, i), 0, 0, 0))),
            scratch_shapes=[pltpu.VMEM((t, 2 * hd), BF16), pltpu.VMEM((t, 2 * hd), BF16),
                            pltpu.VMEM((tq, 2 * hd), BF16),
                            pltpu.VMEM((tq, tq), F32), pltpu.VMEM((tq, tq), F32),
                            pltpu.VMEM((tq, LANES), F32), pltpu.VMEM((tq, LANES), F32),
                            pltpu.VMEM((tq, LANES), F32), pltpu.VMEM((tq, 2 * hd), F32),
                            pltpu.VMEM((2 * pages_per_step,) + cache_k.shape[1:], F32),
                            pltpu.SemaphoreType.DMA((2,))]),
        compiler_params=_params(("arbitrary", "arbitrary", "arbitrary")),
        name="moba_prompt",
    )(page_table.reshape(-1), q, k, v, kmean, cache_k)
    return ya, cmean.reshape((bs, n_pages // per_blk) + cache_k.shape[2:])


def _mlstm_kernel(qk_ref, v_ref, gc_ref, gr_ref, c0_ref, n0_ref, m0_ref,
                  h_ref, c_out, n_out, m_out, c_s, n_s, m_s, *, valid_rows):
    ci = pl.program_id(1)
    L = qk_ref.shape[0]
    wqk = N_HEADS_B * DK_B

    @pl.when(ci == 0)
    def _():
        c_s[...] = c0_ref[...]
        n_s[...] = n0_ref[...]
        m_s[...] = m0_ref[...]

    gc = gc_ref[...]
    gr = gr_ref[...]
    if valid_rows < L:
        rr = lax.broadcasted_iota(jnp.int32, gc.shape, 0)
        ln = lax.broadcasted_iota(jnp.int32, gc.shape, 1)
        gc = jnp.where(rr < valid_rows, gc, jnp.where(ln < N_HEADS_B, MASKED, 0.0))
        cc = lax.broadcasted_iota(jnp.int32, gr.shape, 1)
        rw = lax.broadcasted_iota(jnp.int32, gr.shape, 0)
        gr = jnp.where(cc < valid_rows, gr, jnp.where(rw < N_HEADS_B, MASKED, 0.0))

    r = lax.broadcasted_iota(jnp.int32, (L, L), 0)
    c = lax.broadcasted_iota(jnp.int32, (L, L), 1)
    causal = c <= r
    tri = jnp.where(causal, 1.0, 0.0).astype(BF16)
    triu = jnp.where(r <= c, 1.0, 0.0).astype(BF16)
    bc_all = sum(_dot(tri, p) for p in _split_bf16(gc, 3))
    br_all = sum(_dot(p, triu) for p in _split_bf16(gr, 3))

    def rep(x, width):
        return jnp.concatenate([x] * (width // LANES), axis=1)

    ones = jnp.ones((L, LANES), BF16)
    for hh in range(N_HEADS_B):
        q = qk_ref[:, hh * DK_B:(hh + 1) * DK_B]
        k = qk_ref[:, wqk + hh * DK_B:wqk + (hh + 1) * DK_B]
        v1 = jnp.concatenate([v_ref[:, hh * DV_B:(hh + 1) * DV_B], ones], axis=1)
        i_col = jnp.broadcast_to(gc[:, hh:hh + 1], (L, LANES))
        b_col = jnp.broadcast_to(bc_all[:, N_HEADS_B + hh:N_HEADS_B + hh + 1], (L, LANES))
        i_row = gr[hh:hh + 1, :]
        b_row = br_all[N_HEADS_B + hh:N_HEADS_B + hh + 1, :]
        C = c_s[hh]
        n = n_s[hh]
        m_prev = m_s[hh]

        dmat = jnp.where(causal, rep(b_col, L) - b_row + i_row, MASKED)
        inter = b_col + m_prev
        m_t = jnp.maximum(inter, jnp.broadcast_to(jnp.max(dmat, axis=1, keepdims=True), (L, LANES)))
        w = jnp.exp(dmat - rep(m_t, L))
        a_inter = jnp.exp(inter - m_t)
        qk = (_dot_nt(q, k) * w).astype(BF16)
        cn = jnp.concatenate([C, jnp.broadcast_to(n, (LANES, DK_B))], axis=0).astype(BF16)
        qc = _dot_nt(q, cn)
        qv = _dot(qk, v1)
        num = rep(a_inter, DV_B) * qc[:, :DV_B] + qv[:, :DV_B]
        den = a_inter * qc[:, DV_B:] + qv[:, DV_B:]
        inv = 1.0 / jnp.maximum(jnp.abs(den), jnp.exp(-m_t))
        h_ref[:, hh * DV_B:(hh + 1) * DV_B] = (num * rep(inv, DV_B)).astype(h_ref.dtype)

        b_last = b_col[L - 1:L, :]
        g_col = b_last - b_col + i_col
        m_new = jnp.maximum(b_last + m_prev, jnp.max(g_col, axis=0, keepdims=True))
        decay = jnp.exp(b_last + m_prev - m_new)
        kw = k.astype(F32) * jnp.exp(g_col - m_new)
        c_s[hh] = decay * C + _dot_tn(v1[:, :DV_B], kw.astype(BF16))
        n_s[hh] = decay * n + jnp.sum(kw, axis=0, keepdims=True)
        m_s[hh] = m_new

    @pl.when(ci == pl.num_programs(1) - 1)
    def _():
        c_out[...] = c_s[...]
        n_out[...] = n_s[...]
        m_out[...] = m_s[...]


def _mlstm(qkb, vb, gates_col, gates_row, c0, n0, m0, *, chunk, valid_rows):
    b, t, _ = qkb.shape
    nc = t // chunk
    hb = N_HEADS_B
    st4 = lambda shp: pl.BlockSpec((None,) + shp, lambda bb, ci: (bb, 0, 0, 0))
    kern = functools.partial(_mlstm_kernel, valid_rows=valid_rows)
    return pl.pallas_call(
        kern,
        out_shape=(jax.ShapeDtypeStruct((b, t, hb * DV_B), BF16),
                   jax.ShapeDtypeStruct(c0.shape, F32),
                   jax.ShapeDtypeStruct(n0.shape, F32),
                   jax.ShapeDtypeStruct(m0.shape, F32)),
        grid=(b, nc),
        in_specs=[pl.BlockSpec((None, chunk, qkb.shape[2]), lambda bb, ci: (bb, ci, 0)),
                  pl.BlockSpec((None, chunk, vb.shape[2]), lambda bb, ci: (bb, ci, 0)),
                  pl.BlockSpec((None, chunk, LANES), lambda bb, ci: (bb, ci, 0)),
                  pl.BlockSpec((None, 2 * hb, chunk), lambda bb, ci: (bb * nc + ci, 0, 0)),
                  st4((hb, DV_B, DK_B)), st4((hb, 1, DK_B)), st4((hb, 1, LANES))],
        out_specs=(pl.BlockSpec((None, chunk, hb * DV_B), lambda bb, ci: (bb, ci, 0)),
                   st4((hb, DV_B, DK_B)), st4((hb, 1, DK_B)), st4((hb, 1, LANES))),
        scratch_shapes=[pltpu.VMEM((hb, DV_B, DK_B), F32),
                        pltpu.VMEM((hb, 1, DK_B), F32),
                        pltpu.VMEM((hb, 1, LANES), F32)],
        compiler_params=_params(("parallel", "arbitrary")),
        name="mlstm",
    )(qkb, vb, gates_col, gates_row, c0, n0, m0)


def _ffn_kernel(x_ref, ya_ref, hb_ref, sga_ref, sgob_ref, gtm_ref, shf_ref, scf_ref, gtf_ref,
                gffn_ref, gfin_ref, wout_ref, wup_ref, wconv_ref, bconv_ref, wdown_ref, *rest, seq_rows):
    if seq_rows:
        fix1_ref, fix2_ref, y_ref, a_ref = rest
    else:
        y_ref, tail_ref, carry = rest
    tm = x_ref.shape[0]
    dff = wdown_ref.shape[0]
    merged = (sga_ref[...].astype(F32) * ya_ref[...].astype(F32)
              + sgob_ref[...].astype(F32) * hb_ref[...].astype(F32))
    x1 = x_ref[...] + gtm_ref[...] * _dot(merged.astype(BF16), wout_ref[...])
    ms = jnp.mean(x1 * x1, axis=-1, keepdims=True)
    xn = x1 * lax.rsqrt(ms + RMS_EPS) * gffn_ref[...]
    h = (xn * (1.0 + scf_ref[...]) + shf_ref[...]).astype(BF16)
    a = _dot(h, wup_ref[:, :dff])
    gb = _dot(h, wup_ref[:, dff:])

    row = lax.broadcasted_iota(jnp.int32, a.shape, 0)
    r1 = pltpu.roll(a, 1, axis=0)
    r2 = pltpu.roll(a, 2, axis=0)
    if seq_rows:
        assert seq_rows & (seq_rows - 1) == 0
        t = row & (seq_rows - 1)
        a1 = jnp.where(t >= 1, r1, fix1_ref[...])
        a2 = jnp.where(t >= 2, r2, fix2_ref[...])
        a_ref[...] = a
    else:
        @pl.when(pl.program_id(1) == 0)
        def _():
            carry[...] = jnp.zeros_like(carry)
        prev1 = carry[SUBLANES - 1:SUBLANES, :]
        prev2 = carry[SUBLANES - 2:SUBLANES - 1, :]
        a1 = jnp.where(row == 0, prev1, r1)
        a2 = jnp.where(row == 0, prev2, jnp.where(row == 1, prev1, r2))
        tail = a[tm - SUBLANES:, :]
        carry[...] = tail
        tail_ref[...] = tail
    ac = bconv_ref[...] + a2 * wconv_ref[0:1, :] + a1 * wconv_ref[1:2, :] + a * wconv_ref[2:3, :]
    gelu = 0.5 * ac * (1.0 + jnp.tanh(math.sqrt(2.0 / math.pi) * (ac + 0.044715 * (ac * ac * ac))))
    f = _dot((gelu * gb).astype(BF16), wdown_ref[...])
    x2 = x1 + gtf_ref[...] * f
    ms2 = jnp.mean(x2 * x2, axis=-1, keepdims=True)
    y_ref[...] = x2 * lax.rsqrt(ms2 + RMS_EPS) * gfin_ref[...]


def _ffn(x, ya, hb, sga, sgob, gtm, shf, scf, gtf, g_ffn, g_fin, w_out, w_up, w_conv, b_conv, w_down,
         *, batch, seq_rows=0, fix1=None, fix2=None):
    n, d = x.shape
    dff = w_down.shape[0]
    tm = ROW_TILE
    if seq_rows:
        assert n == tm
        grid = (1, 1)
        mod_spec = pl.BlockSpec((tm, d), lambda bb, i: (0, 0))
        nt = 1
    else:
        nt = n // batch // tm
        grid = (batch, nt)
        mod_spec = pl.BlockSpec((None, 1, d), lambda bb, i: (bb, 0, 0))
    row = lambda w: pl.BlockSpec((tm, w), lambda bb, i: (bb * nt + i, 0))
    in_specs = [row(d), row(d), row(d), row(d), row(d), mod_spec, mod_spec, mod_spec, mod_spec,
                _resident((1, d)), _resident((1, d)), _resident(w_out.shape), _resident(w_up.shape),
                _resident(w_conv.shape), _resident((1, dff)), _resident(w_down.shape)]
    args = [x, ya, hb, sga, sgob, gtm, shf, scf, gtf, g_ffn, g_fin, w_out, w_up, w_conv, b_conv, w_down]
    if seq_rows:
        in_specs += [row(dff), row(dff)]
        args += [fix1, fix2]
        out_shape = (jax.ShapeDtypeStruct((n, d), F32), jax.ShapeDtypeStruct((n, dff), F32))
        out_specs = (row(d), row(dff))
        scratch = []
    else:
        out_shape = (jax.ShapeDtypeStruct((n, d), F32), jax.ShapeDtypeStruct((batch, SUBLANES, dff), F32))
        out_specs = (row(d), pl.BlockSpec((None, SUBLANES, dff), lambda bb, i: (bb, 0, 0)))
        scratch = [pltpu.VMEM((SUBLANES, dff), F32)]
    return pl.pallas_call(
        functools.partial(_ffn_kernel, seq_rows=seq_rows),
        out_shape=out_shape,
        grid=grid,
        in_specs=in_specs,
        out_specs=out_specs,
        scratch_shapes=scratch,
        compiler_params=_params(("parallel", "arbitrary")),
        name="merge_ffn",
    )(*args)


def _select_kernel(q_ref, km_ref, o_ref):
    n_blocks = km_ref.shape[0]
    ts = q_ref.shape[0]
    lane = lax.broadcasted_iota(jnp.int32, (ts, LANES), 1)
    for hh in range(N_HEADS_A):
        sl = slice(hh * HEAD_DIM_A, (hh + 1) * HEAD_DIM_A)
        gate = _gate_scores(q_ref[:, sl].astype(BF16), km_ref[:, hh, :])
        g = gate
        blk = lax.broadcasted_iota(jnp.int32, gate.shape, 1).astype(F32)
        out = jnp.zeros((ts, LANES), F32)
        for r in range(MOBA_TOPK):
            mx = jnp.max(g, axis=1, keepdims=True)
            idx = jnp.min(jnp.where(g == mx, blk, float(n_blocks)), axis=1, keepdims=True)
            out = jnp.where(lane == r, idx, out)
            g = jnp.where(blk == idx, -jnp.inf, g)
        o_ref[hh] = out.astype(jnp.int32)


def _select_blocks(q, kmean):
    b, ts, w = q.shape
    _, nb, nh, hd = kmean.shape
    return pl.pallas_call(
        _select_kernel,
        out_shape=jax.ShapeDtypeStruct((b, N_HEADS_A, ts, LANES), jnp.int32),
        grid=(b,),
        in_specs=[pl.BlockSpec((None, ts, w), lambda bb: (bb, 0, 0)),
                  pl.BlockSpec((None, nb, nh, hd), lambda bb: (bb, 0, 0, 0))],
        out_specs=pl.BlockSpec((None, N_HEADS_A, ts, LANES), lambda bb: (bb, 0, 0, 0)),
        compiler_params=_params(("parallel",)),
        name="select_blocks",
    )(q, kmean)


def _moba_sample_kernel(sel_ref, pt_ref, q_ref, kn_ref, vn_ref, ck_ref, cv_ref, o_ref, kbuf, vbuf, sem,
                        *, n_pages, per_blk):
    b = pl.program_id(0)
    hh = pl.program_id(1)
    ts = q_ref.shape[0]
    page = kbuf.shape[1]
    nsel = MOBA_TOPK * per_blk

    step = b * N_HEADS_A + hh
    nsteps = pl.num_programs(0) * N_HEADS_A
    nslots = ts * nsel

    def copies(stp, t, r, p):
        blk = sel_ref[(stp * ts + t) * MOBA_TOPK + r]
        pg = pt_ref[(stp // N_HEADS_A) * n_pages + blk * per_blk + p]
        head = stp % N_HEADS_A
        half = stp % 2
        slot = half * nslots + (t * MOBA_TOPK + r) * per_blk + p
        return (pltpu.make_async_copy(ck_ref.at[pg, :, head, :], kbuf.at[slot], sem.at[0, half]),
                pltpu.make_async_copy(cv_ref.at[pg, :, head, :], vbuf.at[slot], sem.at[1, half]))

    def for_all(stp, fn):
        for t in range(ts):
            for r in range(MOBA_TOPK):
                for p in range(per_blk):
                    ck, cv = copies(stp, t, r, p)
                    fn(ck)
                    fn(cv)

    @pl.when(step == 0)
    def _():
        for_all(step, lambda c: c.start())

    @pl.when(step + 1 < nsteps)
    def _():
        for_all(step + 1, lambda c: c.start())

    for_all(step, lambda c: c.wait())

    base = (step % 2) * nslots
    qf = q_ref[...]
    q = qf.astype(BF16)
    kn = kn_ref[...]
    vn = vn_ref[...]
    jrow = lax.broadcasted_iota(jnp.int32, (ts, LANES), 0)
    for t in range(ts):
        qrep = jnp.broadcast_to(q[t:t + 1, :], (LANES, q.shape[1]))
        kc = kbuf[pl.ds(base + t * nsel, nsel)].reshape(nsel * page, -1).astype(BF16)
        vc = vbuf[pl.ds(base + t * nsel, nsel)].reshape(nsel * page, -1)
        s = _dot_nt(kc, qrep)
        s_own = jnp.where(jrow <= t, jnp.sum(kn * qf[t:t + 1, :], axis=1, keepdims=True), MASKED)
        m = jnp.maximum(jnp.max(s, axis=0, keepdims=True), jnp.max(s_own, axis=0, keepdims=True))
        p = jnp.exp2(s - m)
        p_own = jnp.exp2(s_own - m)
        l = jnp.sum(p, axis=0, keepdims=True) + jnp.sum(p_own, axis=0, keepdims=True)
        o = jnp.sum(p * vc, axis=0, keepdims=True) + jnp.sum(p_own * vn, axis=0, keepdims=True)
        o_ref[t:t + 1, :] = o / l


def _moba_sample(q, k_new, v_new, cache_k, cache_v, page_table, sel):
    b, n_pages = page_table.shape
    ts = q.shape[0] // b
    _, page, nh, hd = cache_k.shape
    per_blk = MOBA_BLOCK // page
    nslots = ts * MOBA_TOPK * per_blk
    blk = pl.BlockSpec((ts, hd), lambda bb, hh, *_: (bb, hh))
    kern = functools.partial(_moba_sample_kernel, n_pages=n_pages, per_blk=per_blk)
    return pl.pallas_call(
        kern,
        out_shape=jax.ShapeDtypeStruct(q.shape, F32),
        grid_spec=pltpu.PrefetchScalarGridSpec(
            num_scalar_prefetch=2,
            grid=(b, nh),
            in_specs=[blk, blk, blk, pl.BlockSpec(memory_space=pl.ANY), pl.BlockSpec(memory_space=pl.ANY)],
            out_specs=blk,
            scratch_shapes=[pltpu.VMEM((2 * nslots, page, hd), F32), pltpu.VMEM((2 * nslots, page, hd), F32),
                            pltpu.SemaphoreType.DMA((2, 2))]),
        compiler_params=_params(("arbitrary", "arbitrary")),
        name="moba_sample",
    )(sel.reshape(-1), page_table.reshape(-1), q, k_new, v_new, cache_k, cache_v)


def _rope_tables(pos):
    half = HEAD_DIM_A // 2
    inv = ROPE_THETA ** (-jnp.arange(half, dtype=F32) / half)
    ang = pos.astype(F32)[:, None] * inv[None, :]
    cos, sin = jnp.cos(ang), jnp.sin(ang)
    return jnp.concatenate([cos, cos], axis=1), jnp.concatenate([-sin, sin], axis=1)


def _pack_w_in(w_in):
    d = w_in.shape[0]
    a = 3 * N_HEADS_A * HEAD_DIM_A + 2 * N_HEADS_B * DK_B + 2 * N_HEADS_B * DV_B
    ng = 2 * N_HEADS_B
    w_main = jnp.concatenate([w_in[:, :a], w_in[:, a + ng:]], axis=1).astype(BF16)
    w_gate = jnp.zeros((d, LANES), BF16).at[:, :ng].set(w_in[:, a:a + ng].astype(BF16))
    return w_main, w_gate


def _layer(x, c_mod, pos, w, *, cache=None, sample=None):
    (g_mix, w_main, w_gate, b_gate, w_out, g_ffn, w_up, w_conv, b_conv, w_down, g_fin) = w
    b, t, d = x.shape
    n = b * t
    wa = N_HEADS_A * HEAD_DIM_A
    dff = w_down.shape[0]
    hb = N_HEADS_B
    sh_m, sc_m, gt_m, sh_f, sc_f, gt_f = jnp.split(c_mod, 6, axis=-1)
    cos2, sin2 = _rope_tables(pos)
    x2d = x.reshape(n, d)

    if sample is None:
        per_seq = lambda a: a.reshape(b, 1, d)
        q, kf, kb16, vf, vb16, qkb, vbb, sga, sgob, gcol, grow, kmean = _inproj(
            x2d, g_mix, per_seq(sh_m), per_seq(sc_m), cos2, sin2, w_main, w_gate, b_gate,
            rows_per_mod=t, rows_per_pos=t)
        kmean = jnp.pad(kmean.reshape(b, t // MOBA_BLOCK, wa), ((0, 0), (0, LANES - t // MOBA_BLOCK), (0, 0)))
        ya, extra = _moba_prompt(q.reshape(b, t, wa), kb16.reshape(b, t, wa), vb16.reshape(b, t, wa), kmean, *cache)
        ya = ya.reshape(n, wa)
        chunk = MLSTM_CHUNK
        nc = t // chunk
        grow3 = grow.reshape(2 * hb, b * nc, chunk).transpose(1, 0, 2)
        c0 = jnp.zeros((b, hb, DV_B, DK_B), F32)
        n0 = jnp.zeros((b, hb, 1, DK_B), F32)
        m0 = jnp.zeros((b, hb, 1, LANES), F32)
        hbo, c_new, n_new, m_new = _mlstm(qkb.reshape(b, t, -1), vbb.reshape(b, t, -1), gcol.reshape(b, t, LANES),
                                          grow3, c0, n0, m0, chunk=chunk, valid_rows=chunk)
        y, tail = _ffn(x2d, ya, hbo.reshape(n, -1), sga, sgob, per_seq(gt_m), per_seq(sh_f), per_seq(sc_f),
                       per_seq(gt_f), g_ffn, g_fin, w_out, w_up, w_conv, b_conv, w_down, batch=b)
        conv_new = tail[:, SUBLANES - (CONV_W - 1):, :]
    else:
        cache_k, cache_v, page_table, kmean, c_st, n_st, m_st, conv0 = sample
        extra = None
        assert n == ROW_TILE and t >= CONV_W - 1
        past = page_table.shape[1] * cache_k.shape[1]
        assert past % MOBA_BLOCK == 0 and t <= MOBA_BLOCK
        per_row = lambda a: jnp.repeat(a, t, axis=0)
        cos2, sin2 = jnp.tile(cos2, (b, 1)), jnp.tile(sin2, (b, 1))
        q, kf, kb16, vf, vb16, qkb, vbb, sga, sgob, gcol, grow, _ = _inproj(
            x2d, g_mix, per_row(sh_m), per_row(sc_m), cos2, sin2, w_main, w_gate, b_gate,
            rows_per_mod=0, rows_per_pos=n)
        qf = q.astype(F32)
        sel = _select_blocks(qf.reshape(b, t, wa), kmean)[..., :MOBA_TOPK]
        ya = _moba_sample(qf, kf, vf, cache_k, cache_v, page_table, sel).astype(BF16)
        chunk = SAMPLE_CHUNK
        pad = lambda a: jnp.pad(a.reshape(b, t, -1), ((0, 0), (0, chunk - t), (0, 0)))
        grow3 = jnp.pad(grow.reshape(2 * hb, b, t).transpose(1, 0, 2), ((0, 0), (0, 0), (0, chunk - t)))
        m0 = jnp.broadcast_to(m_st[:, :, None, None], (b, hb, 1, LANES))
        hbo, c_new, n_new, m_new = _mlstm(pad(qkb), pad(vbb), pad(gcol), grow3, c_st, n_st[:, :, None, :], m0,
                                          chunk=chunk, valid_rows=t)
        hbo = hbo[:, :t].reshape(n, -1)
        zero = jnp.zeros((b, t - 2, dff), F32)
        fix1 = jnp.concatenate([conv0[:, 1:2], zero, zero[:, :1]], axis=1).reshape(n, dff)
        fix2 = jnp.concatenate([conv0, zero], axis=1).reshape(n, dff)
        y, a_full = _ffn(x2d, ya, hbo, sga, sgob, per_row(gt_m), per_row(sh_f), per_row(sc_f), per_row(gt_f),
                         g_ffn, g_fin, w_out, w_up, w_conv, b_conv, w_down, batch=b, seq_rows=t,
                         fix1=fix1, fix2=fix2)
        conv_new = a_full.reshape(b, t, dff)[:, t - (CONV_W - 1):]
    return (y.reshape(b, t, d), kf.reshape(b, t, N_HEADS_A, HEAD_DIM_A), vf.reshape(b, t, N_HEADS_A, HEAD_DIM_A),
            c_new, n_new[:, :, 0, :], m_new[:, :, 0, 0], conv_new, extra)


def kernel(x_prompt, x_sample, cache_k, cache_v, page_table, state_C, state_n, state_m, state_conv,
           c_prompt, c_sample, w_ada, b_ada, g_norm_mix, w_in, b_gates, w_out, g_norm_ffn,
           w_up, w_conv, b_conv, w_down, g_norm_final):
    depth = w_in.shape[0]
    assert depth == 1
    bp, tp, d = x_prompt.shape
    bs, ts, _ = x_sample.shape
    past = page_table.shape[1] * cache_k.shape[2]
    dl = 0
    mod = _ada(jnp.concatenate([c_prompt, c_sample], axis=0), w_ada[dl], b_ada[dl])
    w_main, w_gate = _pack_w_in(w_in[dl])
    b_gate = jnp.zeros((1, LANES), F32).at[0, :2 * N_HEADS_B].set(b_gates[dl])
    w = (g_norm_mix[dl].reshape(1, d), w_main, w_gate, b_gate, w_out[dl].astype(BF16),
         g_norm_ffn[dl].reshape(1, d), w_up[dl].astype(BF16), w_conv[dl], b_conv[dl].reshape(1, -1),
         w_down[dl].astype(BF16), g_norm_final.reshape(1, d))
    yp, kp, vp, cp, np_, mp, cvp, cache_kmean = _layer(
        x_prompt, mod[:bp], jnp.arange(tp, dtype=jnp.int32), w, cache=(cache_k[dl], page_table))
    ys, ks, vs, cs, ns, ms, cvs, _ = _layer(
        x_sample, mod[bp:], past + jnp.arange(ts, dtype=jnp.int32), w,
        sample=(cache_k[dl], cache_v[dl], page_table, cache_kmean,
                state_C[dl], state_n[dl], state_m[dl], state_conv[dl]))
    st = lambda a: a[None]
    return (yp, ys, st(kp), st(vp), st(ks), st(vs), st(cp), st(np_), st(mp), st(cs), st(ns), st(ms),
            st(cvp), st(cvs))
```

```python
import functools
import math

import jax
import jax.numpy as jnp
from jax import lax
from jax.experimental import pallas as pl
from jax.experimental.pallas import tpu as pltpu

F32 = jnp.float32
BF16 = jnp.bfloat16

N_HEADS_A = 8
HEAD_DIM_A = 128
MOBA_BLOCK = 256
MOBA_TOPK = 3
ROPE_THETA = 10000.0
N_HEADS_B = 4
DK_B = 128
DV_B = 256
CONV_W = 3
RMS_EPS = 1e-6
MASKED = -1e30
LANES = 128
SUBLANES = 8
ROW_TILE = 256
MLSTM_CHUNK = 256
SAMPLE_CHUNK = 128
FFN_CHUNK = 768
VMEM_LIMIT = 56 * 1024 * 1024


def _resident(shape):
    nd = len(shape)
    return pl.BlockSpec(shape, lambda *_: (0,) * nd, pipeline_mode=pl.Buffered(1))


def _params(sem):
    return pltpu.CompilerParams(dimension_semantics=sem, vmem_limit_bytes=VMEM_LIMIT)


def _split_bf16(x, pieces):
    out = []
    r = x
    for _ in range(pieces):
        p = r.astype(BF16)
        out.append(p)
        r = r - p.astype(F32)
    return out


def _dot_nt(a, b):
    return lax.dot_general(a, b, (((1,), (1,)), ((), ())), preferred_element_type=F32)


def _dot_tn(a, b):
    return lax.dot_general(a, b, (((0,), (0,)), ((), ())), preferred_element_type=F32)


def _dot(a, b):
    return jnp.dot(a, b, preferred_element_type=F32)


def _ada_kernel(c_ref, w_ref, b_ref, o_ref):
    c = c_ref[...]
    s = (c * jax.nn.sigmoid(c)).astype(BF16)
    o_ref[...] = _dot(s, w_ref[...].astype(BF16)) + b_ref[...]


def _ada(c, w_ada, b_ada):
    n, d = c.shape
    d_out = w_ada.shape[1]
    tn = d
    return pl.pallas_call(
        _ada_kernel,
        out_shape=jax.ShapeDtypeStruct((n, d_out), F32),
        grid=(d_out // tn,),
        in_specs=[pl.BlockSpec((n, d), lambda j: (0, 0)),
                  pl.BlockSpec((d, tn), lambda j: (0, j)),
                  pl.BlockSpec((1, tn), lambda j: (0, j))],
        out_specs=pl.BlockSpec((n, tn), lambda j: (0, j)),
        compiler_params=_params(("arbitrary",)),
        name="ada_mod",
    )(c, w_ada, b_ada.reshape(1, d_out))


def _inproj_kernel(x_ref, g_ref, sh_ref, sc_ref, cos_ref, sin_ref, w_ref, wg_ref, bg_ref,
                   q_ref, kf_ref, kb_ref, vf_ref, vb_ref, qkb_ref, vbb_ref, sga_ref, sgob_ref,
                   gc_ref, gr_ref, km_ref):
    tm, d = x_ref.shape
    wa = N_HEADS_A * HEAD_DIM_A
    x = x_ref[...]
    ms = jnp.mean(x * x, axis=-1, keepdims=True)
    xn = x * lax.rsqrt(ms + RMS_EPS) * g_ref[...]
    h = (xn * (1.0 + sc_ref[...]) + sh_ref[...]).astype(BF16)
    cos = cos_ref[...]
    sin = sin_ref[...]

    def proj(c):
        return _dot(h, w_ref[:, c * wa:(c + 1) * wa])

    def rope(a, hh):
        xh = a[:, hh * HEAD_DIM_A:(hh + 1) * HEAD_DIM_A]
        return xh * cos + pltpu.roll(xh, HEAD_DIM_A // 2, axis=1) * sin

    qscale = HEAD_DIM_A ** -0.5 * math.log2(math.e)
    a = proj(0)
    for hh in range(N_HEADS_A):
        sl = slice(hh * HEAD_DIM_A, (hh + 1) * HEAD_DIM_A)
        q_ref[:, sl] = (rope(a, hh) * qscale).astype(BF16)
    a = proj(1)
    for hh in range(N_HEADS_A):
        sl = slice(hh * HEAD_DIM_A, (hh + 1) * HEAD_DIM_A)
        kr = rope(a, hh)
        kf_ref[:, sl] = kr
        kb_ref[:, sl] = kr.astype(BF16)
        for g in range(tm // MOBA_BLOCK):
            km_ref[g, :, sl] = jnp.mean(kr[g * MOBA_BLOCK:(g + 1) * MOBA_BLOCK], axis=0, keepdims=True)
    a = proj(2)
    vf_ref[...] = a
    vb_ref[...] = a.astype(BF16)
    a = proj(3)
    wqk = N_HEADS_B * DK_B
    qkb_ref[:, :wqk] = a[:, :wqk].astype(BF16)
    qkb_ref[:, wqk:] = (a[:, wqk:] * (DK_B ** -0.5)).astype(BF16)
    vbb_ref[...] = proj(4).astype(BF16)
    so = jax.nn.sigmoid(proj(5))
    sga_ref[...] = jax.nn.sigmoid(proj(6)).astype(BF16)
    sgob_ref[...] = (jax.nn.sigmoid(proj(7)) * so).astype(BF16)
    g = _dot(h, wg_ref[...]) + bg_ref[...]
    lane = lax.broadcasted_iota(jnp.int32, g.shape, 1)
    logsig = jnp.minimum(g, 0.0) - jnp.log(1.0 + jnp.exp(-jnp.abs(g)))
    g = jnp.where(lane >= N_HEADS_B, logsig, g)
    gc_ref[...] = g
    gr_ref[...] = g.T[:2 * N_HEADS_B, :]


def _inproj(x, g_mix, shift, scale, cos2, sin2, w_main, w_gate, b_gate, *, rows_per_mod, rows_per_pos):
    n, d = x.shape
    tm = ROW_TILE
    assert n % tm == 0 and tm % MOBA_BLOCK == 0
    wa = N_HEADS_A * HEAD_DIM_A
    if rows_per_mod:
        assert rows_per_mod % tm == 0
        mod_spec = pl.BlockSpec((None, 1, d), lambda i: (i // (rows_per_mod // tm), 0, 0))
    else:
        mod_spec = pl.BlockSpec((tm, d), lambda i: (i, 0))
    assert rows_per_pos % tm == 0
    pos_spec = pl.BlockSpec((tm, HEAD_DIM_A), lambda i: (i % (rows_per_pos // tm), 0))
    row = lambda w: pl.BlockSpec((tm, w), lambda i: (i, 0))
    nblk = tm // MOBA_BLOCK
    out_shape = (
        jax.ShapeDtypeStruct((n, wa), BF16),
        jax.ShapeDtypeStruct((n, wa), F32),
        jax.ShapeDtypeStruct((n, wa), BF16),
        jax.ShapeDtypeStruct((n, wa), F32),
        jax.ShapeDtypeStruct((n, wa), BF16),
        jax.ShapeDtypeStruct((n, 2 * N_HEADS_B * DK_B), BF16),
        jax.ShapeDtypeStruct((n, N_HEADS_B * DV_B), BF16),
        jax.ShapeDtypeStruct((n, wa), BF16),
        jax.ShapeDtypeStruct((n, N_HEADS_B * DV_B), BF16),
        jax.ShapeDtypeStruct((n, LANES), F32),
        jax.ShapeDtypeStruct((2 * N_HEADS_B, n), F32),
        jax.ShapeDtypeStruct((n // MOBA_BLOCK, 1, wa), F32),
    )
    out_specs = (row(wa), row(wa), row(wa), row(wa), row(wa), row(2 * N_HEADS_B * DK_B), row(N_HEADS_B * DV_B),
                 row(wa), row(N_HEADS_B * DV_B), row(LANES),
                 pl.BlockSpec((2 * N_HEADS_B, tm), lambda i: (0, i)),
                 pl.BlockSpec((nblk, 1, wa), lambda i: (i, 0, 0)))
    return pl.pallas_call(
        _inproj_kernel,
        out_shape=out_shape,
        grid=(n // tm,),
        in_specs=[row(d), _resident((1, d)), mod_spec, mod_spec, pos_spec, pos_spec,
                  _resident(w_main.shape), _resident(w_gate.shape), _resident((1, LANES))],
        out_specs=out_specs,
        compiler_params=_params(("parallel",)),
        name="in_proj",
    )(x, g_mix, shift, scale, cos2, sin2, w_main, w_gate, b_gate)


def _topk_mask(gate, valid, axis):
    blk = lax.broadcasted_iota(jnp.int32, gate.shape, axis).astype(F32)
    g = jnp.where(valid, gate, -jnp.inf)
    sel = jnp.zeros(gate.shape, F32)
    for _ in range(MOBA_TOPK):
        mx = jnp.max(g, axis=axis, keepdims=True)
        idx = jnp.min(jnp.where(g == mx, blk, float(gate.shape[axis])), axis=axis, keepdims=True)
        pick = blk == idx
        sel = jnp.where(pick, jnp.where(valid, 1.0, 0.0), sel)
        g = jnp.where(pick, -jnp.inf, g)
    return sel


def _gate_scores(q, kmean):
    hi, lo = _split_bf16(kmean, 2)
    return _dot_nt(q, hi) + _dot_nt(q, lo)


def _gate_scores_t(q, kmean):
    hi, lo = _split_bf16(kmean, 2)
    return _dot_nt(hi, q) + _dot_nt(lo, q)


def _moba_prompt_kernel(pt_ref, q_ref, k_ref, v_ref, km_ref, ck_ref, o_ref, cm_ref,
                        kaug, vext, qa, s0, s1, mx0, mx1, m_ref, acc_ref, pbuf, psem, *, pages_per_step):
    i = pl.program_id(2)
    tq, hd = q_ref.shape
    t = k_ref.shape[0]
    step = (pl.program_id(0) * pl.num_programs(1) + pl.program_id(1)) * pl.num_programs(2) + i
    n_steps = pl.num_programs(0) * pl.num_programs(1) * pl.num_programs(2)

    def page_copy(stp, p):
        half = stp % 2
        return pltpu.make_async_copy(ck_ref.at[pt_ref[stp * pages_per_step + p]],
                                     pbuf.at[half * pages_per_step + p], psem.at[half])

    @pl.when(i == 0)
    def _():
        vext[:, :hd] = v_ref[...]
        vext[:, hd:] = jnp.ones((t, hd), BF16)
        rowi = lax.broadcasted_iota(jnp.int32, (hd, tq), 0)
        coli = lax.broadcasted_iota(jnp.int32, (hd, tq), 1)
        for j in range(t // tq):
            kaug[:hd, j * tq:(j + 1) * tq] = k_ref[j * tq:(j + 1) * tq, :].T
            blk_of_key = j * (tq // MOBA_BLOCK) + coli // MOBA_BLOCK
            kaug[hd:, j * tq:(j + 1) * tq] = jnp.where(rowi == blk_of_key, 1.0, 0.0).astype(BF16)

    @pl.when(step == 0)
    def _():
        for p in range(pages_per_step):
            page_copy(step, p).start()

    @pl.when(step + 1 < n_steps)
    def _():
        for p in range(pages_per_step):
            page_copy(step + 1, p).start()

    for p in range(pages_per_step):
        page_copy(step, p).wait()
    per_blk = MOBA_BLOCK // pbuf.shape[1]
    for r in range(pages_per_step // per_blk):
        base = (step % 2) * pages_per_step + r * per_blk
        tot = jnp.sum(pbuf[base], axis=0)
        for p in range(1, per_blk):
            tot = tot + jnp.sum(pbuf[base + p], axis=0)
        cm_ref[r] = tot * (1.0 / MOBA_BLOCK)

    q = q_ref[...]
    nbp = -(-(t // MOBA_BLOCK) // SUBLANES) * SUBLANES
    gate = _gate_scores_t(q, km_ref[:nbp, :])
    blk = lax.broadcasted_iota(jnp.int32, gate.shape, 0)
    own = i * (tq // MOBA_BLOCK) + lax.broadcasted_iota(jnp.int32, gate.shape, 1) // MOBA_BLOCK
    sel = _topk_mask(gate, blk < own, 0)
    bias = jnp.where(blk == own, 0.0, jnp.where(sel > 0.5, 0.0, MASKED))
    bias = jnp.concatenate([bias, jnp.zeros((LANES - nbp, tq), F32)], axis=0)
    qa[:, :hd] = q
    qa[:, hd:] = bias.T.astype(BF16)
    m_ref[...] = jnp.full(m_ref.shape, MASKED, F32)
    acc_ref[...] = jnp.zeros(acc_ref.shape, F32)

    n_lt = tq // LANES

    def lane_tile_max(s):
        pm = s[:, :LANES]
        for j in range(1, n_lt):
            pm = jnp.maximum(pm, s[:, j * LANES:(j + 1) * LANES])
        return pm

    def scores_to(slot, e):
        s_ref, mx_ref = slot
        st = pl.multiple_of(e * tq, tq)
        s = _dot(qa[...], kaug[:, pl.ds(st, tq)])
        s_ref[...] = s
        mx_ref[...] = lane_tile_max(s)

    def absorb(slot, e):
        s_ref, mx_ref = slot
        st = pl.multiple_of(e * tq, tq)
        m = m_ref[...]
        m_new = jnp.maximum(m, jnp.broadcast_to(jnp.max(mx_ref[...], axis=1, keepdims=True), m.shape))
        p = jnp.concatenate([jnp.exp2(s_ref[:, j * LANES:(j + 1) * LANES] - m_new).astype(BF16)
                             for j in range(n_lt)], axis=1)
        alpha = jnp.exp2(m - m_new)
        pv = _dot(p, vext[pl.ds(st, tq), :])
        acc_ref[:, :hd] = alpha * acc_ref[:, :hd] + pv[:, :hd]
        acc_ref[:, hd:] = alpha * acc_ref[:, hd:] + pv[:, hd:]
        m_ref[...] = m_new

    def absorb_own(slot):
        s_ref, mx_ref = slot
        r = lax.broadcasted_iota(jnp.int32, s_ref.shape, 0)
        c = lax.broadcasted_iota(jnp.int32, s_ref.shape, 1)
        s = jnp.where(c <= r, s_ref[...], MASKED)
        s_ref[...] = s
        mx_ref[...] = lane_tile_max(s)
        absorb(slot, i)
        o_ref[...] = (acc_ref[:, :hd] / acc_ref[:, hd:]).astype(o_ref.dtype)

    slot0, slot1 = (s0, mx0), (s1, mx1)
    scores_to(slot0, 0)

    def pair(p, carry):
        e = 2 * p
        scores_to(slot1, e + 1)
        absorb(slot0, e)
        scores_to(slot0, e + 2)
        absorb(slot1, e + 1)
        return carry

    lax.fori_loop(0, i // 2, pair, 0)

    @pl.when(i % 2 == 0)
    def _():
        absorb_own(slot0)

    @pl.when(i % 2 == 1)
    def _():
        scores_to(slot1, i)
        absorb(slot0, i - 1)
        absorb_own(slot1)


def _moba_prompt(q, k, v, kmean, cache_k, page_table, *, blocks_per_tile=2):
    b, t, w = q.shape
    tq = blocks_per_tile * MOBA_BLOCK
    hd = HEAD_DIM_A
    nh = w // hd
    assert t % tq == 0 and t // MOBA_BLOCK <= LANES and kmean.shape[1] == LANES
    n_steps = b * nh * (t // tq)
    bs, n_pages = page_table.shape
    page = cache_k.shape[1]
    per_blk = MOBA_BLOCK // page
    pages_per_step = bs * n_pages // n_steps
    assert pages_per_step * n_steps == bs * n_pages and pages_per_step % per_blk == 0 and MOBA_BLOCK % page == 0
    blk_per_step = pages_per_step // per_blk

    def step_of(bb, hh, i):
        return (bb * nh + hh) * (t // tq) + i

    ya, cmean = pl.pallas_call(
        functools.partial(_moba_prompt_kernel, pages_per_step=pages_per_step),
        out_shape=(jax.ShapeDtypeStruct((b, t, w), BF16),
                   jax.ShapeDtypeStruct((n_steps, blk_per_step) + cache_k.shape[2:], F32)),
        grid_spec=pltpu.PrefetchScalarGridSpec(
            num_scalar_prefetch=1,
            grid=(b, nh, t // tq),
            in_specs=[pl.BlockSpec((None, tq, hd), lambda bb, hh, i, pt: (bb, i, hh)),
                      pl.BlockSpec((None, t, hd), lambda bb, hh, i, pt: (bb, 0, hh)),
                      pl.BlockSpec((None, t, hd), lambda bb, hh, i, pt: (bb, 0, hh)),
                      pl.BlockSpec((None, LANES, hd), lambda bb, hh, i, pt: (bb, 0, hh)),
                      pl.BlockSpec(memory_space=pl.ANY)],
            out_specs=(pl.BlockSpec((None, tq, hd), lambda bb, hh, i, pt: (bb, i, hh)),
                       pl.BlockSpec((None, blk_per_step) + cache_k.shape[2:],
                                    lambda bb, hh, i, pt: (step_of(bb, hh, i), 0, 0, 0))),
            scratch_shapes=[pltpu.VMEM((2 * hd, t), BF16), pltpu.VMEM((t, 2 * hd), BF16),
                            pltpu.VMEM((tq, 2 * hd), BF16),
                            pltpu.VMEM((tq, tq), F32), pltpu.VMEM((tq, tq), F32),
                            pltpu.VMEM((tq, LANES), F32), pltpu.VMEM((tq, LANES), F32),
                            pltpu.VMEM((tq, LANES), F32), pltpu.VMEM((tq, 2 * hd), F32),
                            pltpu.VMEM((2 * pages_per_step,) + cache_k.shape[1:], F32),
                            pltpu.SemaphoreType.DMA((2,))]),
        compiler_params=_params(("arbitrary", "arbitrary", "arbitrary")),
        name="moba_prompt",
    )(page_table.reshape(-1), q, k, v, kmean, cache_k)
    return ya, cmean.reshape((bs, n_pages // per_blk) + cache_k.shape[2:])


def _mlstm_kernel(qk_ref, v_ref, gc_ref, gr_ref, c0_ref, n0_ref, m0_ref,
                  h_ref, c_out, n_out, m_out, c_s, n_s, m_s, *, valid_rows):
    ci = pl.program_id(1)
    L = qk_ref.shape[0]
    wqk = N_HEADS_B * DK_B

    @pl.when(ci == 0)
    def _():
        c_s[...] = c0_ref[...]
        n_s[...] = n0_ref[...]
        m_s[...] = m0_ref[...]

    gc = gc_ref[...]
    gr = gr_ref[...]
    if valid_rows < L:
        rr = lax.broadcasted_iota(jnp.int32, gc.shape, 0)
        ln = lax.broadcasted_iota(jnp.int32, gc.shape, 1)
        gc = jnp.where(rr < valid_rows, gc, jnp.where(ln < N_HEADS_B, MASKED, 0.0))
        cc = lax.broadcasted_iota(jnp.int32, gr.shape, 1)
        rw = lax.broadcasted_iota(jnp.int32, gr.shape, 0)
        gr = jnp.where(cc < valid_rows, gr, jnp.where(rw < N_HEADS_B, MASKED, 0.0))

    r = lax.broadcasted_iota(jnp.int32, (L, L), 0)
    c = lax.broadcasted_iota(jnp.int32, (L, L), 1)
    causal = c <= r
    tri = jnp.where(causal, 1.0, 0.0).astype(BF16)
    triu = jnp.where(r <= c, 1.0, 0.0).astype(BF16)
    bc_all = sum(_dot(tri, p) for p in _split_bf16(gc, 3))
    br_all = sum(_dot(p, triu) for p in _split_bf16(gr, 3))

    def rep(x, width):
        return jnp.concatenate([x] * (width // LANES), axis=1)

    ones = jnp.ones((L, LANES), BF16)
    for hh in range(N_HEADS_B):
        q = qk_ref[:, hh * DK_B:(hh + 1) * DK_B]
        k = qk_ref[:, wqk + hh * DK_B:wqk + (hh + 1) * DK_B]
        v1 = jnp.concatenate([v_ref[:, hh * DV_B:(hh + 1) * DV_B], ones], axis=1)
        i_col = jnp.broadcast_to(gc[:, hh:hh + 1], (L, LANES))
        b_col = jnp.broadcast_to(bc_all[:, N_HEADS_B + hh:N_HEADS_B + hh + 1], (L, LANES))
        i_row = gr[hh:hh + 1, :]
        b_row = br_all[N_HEADS_B + hh:N_HEADS_B + hh + 1, :]
        C = c_s[hh]
        n = n_s[hh]
        m_prev = m_s[hh]

        dmat = jnp.where(causal, rep(b_col, L) - b_row + i_row, MASKED)
        inter = b_col + m_prev
        m_t = jnp.maximum(inter, jnp.broadcast_to(jnp.max(dmat, axis=1, keepdims=True), (L, LANES)))
        w = jnp.exp(dmat - rep(m_t, L))
        a_inter = jnp.exp(inter - m_t)
        qk = (_dot_nt(q, k) * w).astype(BF16)
        cn = jnp.concatenate([C, jnp.broadcast_to(n, (LANES, DK_B))], axis=0).astype(BF16)
        qc = _dot_nt(q, cn)
        qv = _dot(qk, v1)
        num = rep(a_inter, DV_B) * qc[:, :DV_B] + qv[:, :DV_B]
        den = a_inter * qc[:, DV_B:] + qv[:, DV_B:]
        inv = 1.0 / jnp.maximum(jnp.abs(den), jnp.exp(-m_t))
        h_ref[:, hh * DV_B:(hh + 1) * DV_B] = (num * rep(inv, DV_B)).astype(h_ref.dtype)

        b_last = b_col[L - 1:L, :]
        g_col = b_last - b_col + i_col
        m_new = jnp.maximum(b_last + m_prev, jnp.max(g_col, axis=0, keepdims=True))
        decay = jnp.exp(b_last + m_prev - m_new)
        kw = k.astype(F32) * jnp.exp(g_col - m_new)
        c_s[hh] = decay * C + _dot_tn(v1[:, :DV_B], kw.astype(BF16))
        n_s[hh] = decay * n + jnp.sum(kw, axis=0, keepdims=True)
        m_s[hh] = m_new

    @pl.when(ci == pl.num_programs(1) - 1)
    def _():
        c_out[...] = c_s[...]
        n_out[...] = n_s[...]
        m_out[...] = m_s[...]


def _mlstm(qkb, vb, gates_col, gates_row, c0, n0, m0, *, chunk, valid_rows):
    b, t, _ = qkb.shape
    nc = t // chunk
    hb = N_HEADS_B
    st4 = lambda shp: pl.BlockSpec((None,) + shp, lambda bb, ci: (bb, 0, 0, 0))
    kern = functools.partial(_mlstm_kernel, valid_rows=valid_rows)
    return pl.pallas_call(
        kern,
        out_shape=(jax.ShapeDtypeStruct((b, t, hb * DV_B), BF16),
                   jax.ShapeDtypeStruct(c0.shape, F32),
                   jax.ShapeDtypeStruct(n0.shape, F32),
                   jax.ShapeDtypeStruct(m0.shape, F32)),
        grid=(b, nc),
        in_specs=[pl.BlockSpec((None, chunk, qkb.shape[2]), lambda bb, ci: (bb, ci, 0)),
                  pl.BlockSpec((None, chunk, vb.shape[2]), lambda bb, ci: (bb, ci, 0)),
                  pl.BlockSpec((None, chunk, LANES), lambda bb, ci: (bb, ci, 0)),
                  pl.BlockSpec((None, 2 * hb, chunk), lambda bb, ci: (bb * nc + ci, 0, 0)),
                  st4((hb, DV_B, DK_B)), st4((hb, 1, DK_B)), st4((hb, 1, LANES))],
        out_specs=(pl.BlockSpec((None, chunk, hb * DV_B), lambda bb, ci: (bb, ci, 0)),
                   st4((hb, DV_B, DK_B)), st4((hb, 1, DK_B)), st4((hb, 1, LANES))),
        scratch_shapes=[pltpu.VMEM((hb, DV_B, DK_B), F32),
                        pltpu.VMEM((hb, 1, DK_B), F32),
                        pltpu.VMEM((hb, 1, LANES), F32)],
        compiler_params=_params(("parallel", "arbitrary")),
        name="mlstm",
    )(qkb, vb, gates_col, gates_row, c0, n0, m0)


def _ffn_kernel(x_ref, ya_ref, hb_ref, sga_ref, sgob_ref, gtm_ref, shf_ref, scf_ref, gtf_ref,
                gffn_ref, gfin_ref, wout_ref, wup_ref, wconv_ref, bconv_ref, wdown_ref, *rest, seq_rows):
    if seq_rows:
        fix1_ref, fix2_ref, y_ref, a_ref = rest
    else:
        y_ref, tail_ref, carry = rest

        @pl.when(pl.program_id(1) == 0)
        def _():
            carry[...] = jnp.zeros_like(carry)
    tm = x_ref.shape[0]
    dff = wdown_ref.shape[0]
    merged = (sga_ref[...].astype(F32) * ya_ref[...].astype(F32)
              + sgob_ref[...].astype(F32) * hb_ref[...].astype(F32))
    x1 = x_ref[...] + gtm_ref[...] * _dot(merged.astype(BF16), wout_ref[...])
    ms = jnp.mean(x1 * x1, axis=-1, keepdims=True)
    xn = x1 * lax.rsqrt(ms + RMS_EPS) * gffn_ref[...]
    h = (xn * (1.0 + scf_ref[...]) + shf_ref[...]).astype(BF16)

    edges = list(range(0, dff, FFN_CHUNK)) + [dff]
    chunks = list(zip(edges[:-1], edges[1:]))

    def up(c0, c1):
        return _dot(h, wup_ref[:, c0:c1]), _dot(h, wup_ref[:, dff + c0:dff + c1])

    nxt = up(*chunks[0])
    f = None
    for ci, (c0, c1) in enumerate(chunks):
        a, gb = nxt
        if ci + 1 < len(chunks):
            nxt = up(*chunks[ci + 1])
        row = lax.broadcasted_iota(jnp.int32, a.shape, 0)
        r1 = pltpu.roll(a, 1, axis=0)
        r2 = pltpu.roll(a, 2, axis=0)
        if seq_rows:
            assert seq_rows & (seq_rows - 1) == 0
            t = row & (seq_rows - 1)
            a1 = jnp.where(t >= 1, r1, fix1_ref[:, c0:c1])
            a2 = jnp.where(t >= 2, r2, fix2_ref[:, c0:c1])
            a_ref[:, c0:c1] = a
        else:
            prev1 = carry[SUBLANES - 1:SUBLANES, c0:c1]
            prev2 = carry[SUBLANES - 2:SUBLANES - 1, c0:c1]
            a1 = jnp.where(row == 0, prev1, r1)
            a2 = jnp.where(row == 0, prev2, jnp.where(row == 1, prev1, r2))
            tail = a[tm - SUBLANES:, :]
            carry[:, c0:c1] = tail
            tail_ref[:, c0:c1] = tail
        ac = (bconv_ref[:, c0:c1] + a2 * wconv_ref[0:1, c0:c1] + a1 * wconv_ref[1:2, c0:c1]
              + a * wconv_ref[2:3, c0:c1])
        gelu = 0.5 * ac * (1.0 + jnp.tanh(math.sqrt(2.0 / math.pi) * (ac + 0.044715 * (ac * ac * ac))))
        part = _dot((gelu * gb).astype(BF16), wdown_ref[c0:c1, :])
        f = part if f is None else f + part
    x2 = x1 + gtf_ref[...] * f
    ms2 = jnp.mean(x2 * x2, axis=-1, keepdims=True)
    y_ref[...] = x2 * lax.rsqrt(ms2 + RMS_EPS) * gfin_ref[...]


def _ffn(x, ya, hb, sga, sgob, gtm, shf, scf, gtf, g_ffn, g_fin, w_out, w_up, w_conv, b_conv, w_down,
         *, batch, seq_rows=0, fix1=None, fix2=None):
    n, d = x.shape
    dff = w_down.shape[0]
    tm = ROW_TILE
    if seq_rows:
        assert n == tm
        grid = (1, 1)
        mod_spec = pl.BlockSpec((tm, d), lambda bb, i: (0, 0))
        nt = 1
    else:
        nt = n // batch // tm
        grid = (batch, nt)
        mod_spec = pl.BlockSpec((None, 1, d), lambda bb, i: (bb, 0, 0))
    row = lambda w: pl.BlockSpec((tm, w), lambda bb, i: (bb * nt + i, 0))
    in_specs = [row(d), row(d), row(d), row(d), row(d), mod_spec, mod_spec, mod_spec, mod_spec,
                _resident((1, d)), _resident((1, d)), _resident(w_out.shape), _resident(w_up.shape),
                _resident(w_conv.shape), _resident((1, dff)), _resident(w_down.shape)]
    args = [x, ya, hb, sga, sgob, gtm, shf, scf, gtf, g_ffn, g_fin, w_out, w_up, w_conv, b_conv, w_down]
    if seq_rows:
        in_specs += [row(dff), row(dff)]
        args += [fix1, fix2]
        out_shape = (jax.ShapeDtypeStruct((n, d), F32), jax.ShapeDtypeStruct((n, dff), F32))
        out_specs = (row(d), row(dff))
        scratch = []
    else:
        out_shape = (jax.ShapeDtypeStruct((n, d), F32), jax.ShapeDtypeStruct((batch, SUBLANES, dff), F32))
        out_specs = (row(d), pl.BlockSpec((None, SUBLANES, dff), lambda bb, i: (bb, 0, 0)))
        scratch = [pltpu.VMEM((SUBLANES, dff), F32)]
    return pl.pallas_call(
        functools.partial(_ffn_kernel, seq_rows=seq_rows),
        out_shape=out_shape,
        grid=grid,
        in_specs=in_specs,
        out_specs=out_specs,
        scratch_shapes=scratch,
        compiler_params=_params(("parallel", "arbitrary")),
        name="merge_ffn",
    )(*args)


def _select_kernel(q_ref, km_ref, o_ref):
    n_blocks = km_ref.shape[0]
    ts = q_ref.shape[0]
    lane = lax.broadcasted_iota(jnp.int32, (ts, LANES), 1)
    for hh in range(N_HEADS_A):
        sl = slice(hh * HEAD_DIM_A, (hh + 1) * HEAD_DIM_A)
        gate = _gate_scores(q_ref[:, sl].astype(BF16), km_ref[:, hh, :])
        g = gate
        blk = lax.broadcasted_iota(jnp.int32, gate.shape, 1).astype(F32)
        out = jnp.zeros((ts, LANES), F32)
        for r in range(MOBA_TOPK):
            mx = jnp.max(g, axis=1, keepdims=True)
            idx = jnp.min(jnp.where(g == mx, blk, float(n_blocks)), axis=1, keepdims=True)
            out = jnp.where(lane == r, idx, out)
            g = jnp.where(blk == idx, -jnp.inf, g)
        o_ref[hh] = out.astype(jnp.int32)


def _select_blocks(q, kmean):
    b, ts, w = q.shape
    _, nb, nh, hd = kmean.shape
    return pl.pallas_call(
        _select_kernel,
        out_shape=jax.ShapeDtypeStruct((b, N_HEADS_A, ts, LANES), jnp.int32),
        grid=(b,),
        in_specs=[pl.BlockSpec((None, ts, w), lambda bb: (bb, 0, 0)),
                  pl.BlockSpec((None, nb, nh, hd), lambda bb: (bb, 0, 0, 0))],
        out_specs=pl.BlockSpec((None, N_HEADS_A, ts, LANES), lambda bb: (bb, 0, 0, 0)),
        compiler_params=_params(("parallel",)),
        name="select_blocks",
    )(q, kmean)


def _moba_sample_kernel(sel_ref, pt_ref, q_ref, kn_ref, vn_ref, ck_ref, cv_ref, o_ref, kbuf, vbuf, sem,
                        *, n_pages, per_blk):
    b = pl.program_id(0)
    hh = pl.program_id(1)
    ts = q_ref.shape[0]
    page = kbuf.shape[1]
    nsel = MOBA_TOPK * per_blk

    step = b * N_HEADS_A + hh
    nsteps = pl.num_programs(0) * N_HEADS_A
    nslots = ts * nsel

    def copies(stp, t, r, p):
        blk = sel_ref[(stp * ts + t) * MOBA_TOPK + r]
        pg = pt_ref[(stp // N_HEADS_A) * n_pages + blk * per_blk + p]
        head = stp % N_HEADS_A
        half = stp % 2
        slot = half * nslots + (t * MOBA_TOPK + r) * per_blk + p
        return (pltpu.make_async_copy(ck_ref.at[pg, :, head, :], kbuf.at[slot], sem.at[0, half]),
                pltpu.make_async_copy(cv_ref.at[pg, :, head, :], vbuf.at[slot], sem.at[1, half]))

    def for_all(stp, fn):
        for t in range(ts):
            for r in range(MOBA_TOPK):
                for p in range(per_blk):
                    ck, cv = copies(stp, t, r, p)
                    fn(ck)
                    fn(cv)

    @pl.when(step == 0)
    def _():
        for_all(step, lambda c: c.start())

    @pl.when(step + 1 < nsteps)
    def _():
        for_all(step + 1, lambda c: c.start())

    for_all(step, lambda c: c.wait())

    base = (step % 2) * nslots
    qf = q_ref[...]
    q = qf.astype(BF16)
    kn = kn_ref[...]
    vn = vn_ref[...]
    jrow = lax.broadcasted_iota(jnp.int32, (ts, LANES), 0)
    for t in range(ts):
        qrep = jnp.broadcast_to(q[t:t + 1, :], (LANES, q.shape[1]))
        kc = kbuf[pl.ds(base + t * nsel, nsel)].reshape(nsel * page, -1).astype(BF16)
        vc = vbuf[pl.ds(base + t * nsel, nsel)].reshape(nsel * page, -1)
        s = _dot_nt(kc, qrep)
        s_own = jnp.where(jrow <= t, jnp.sum(kn * qf[t:t + 1, :], axis=1, keepdims=True), MASKED)
        m = jnp.maximum(jnp.max(s, axis=0, keepdims=True), jnp.max(s_own, axis=0, keepdims=True))
        p = jnp.exp2(s - m)
        p_own = jnp.exp2(s_own - m)
        l = jnp.sum(p, axis=0, keepdims=True) + jnp.sum(p_own, axis=0, keepdims=True)
        o = jnp.sum(p * vc, axis=0, keepdims=True) + jnp.sum(p_own * vn, axis=0, keepdims=True)
        o_ref[t:t + 1, :] = o / l


def _moba_sample(q, k_new, v_new, cache_k, cache_v, page_table, sel):
    b, n_pages = page_table.shape
    ts = q.shape[0] // b
    _, page, nh, hd = cache_k.shape
    per_blk = MOBA_BLOCK // page
    nslots = ts * MOBA_TOPK * per_blk
    blk = pl.BlockSpec((ts, hd), lambda bb, hh, *_: (bb, hh))
    kern = functools.partial(_moba_sample_kernel, n_pages=n_pages, per_blk=per_blk)
    return pl.pallas_call(
        kern,
        out_shape=jax.ShapeDtypeStruct(q.shape, F32),
        grid_spec=pltpu.PrefetchScalarGridSpec(
            num_scalar_prefetch=2,
            grid=(b, nh),
            in_specs=[blk, blk, blk, pl.BlockSpec(memory_space=pl.ANY), pl.BlockSpec(memory_space=pl.ANY)],
            out_specs=blk,
            scratch_shapes=[pltpu.VMEM((2 * nslots, page, hd), F32), pltpu.VMEM((2 * nslots, page, hd), F32),
                            pltpu.SemaphoreType.DMA((2, 2))]),
        compiler_params=_params(("arbitrary", "arbitrary")),
        name="moba_sample",
    )(sel.reshape(-1), page_table.reshape(-1), q, k_new, v_new, cache_k, cache_v)


def _rope_tables(pos):
    half = HEAD_DIM_A // 2
    inv = ROPE_THETA ** (-jnp.arange(half, dtype=F32) / half)
    ang = pos.astype(F32)[:, None] * inv[None, :]
    cos, sin = jnp.cos(ang), jnp.sin(ang)
    return jnp.concatenate([cos, cos], axis=1), jnp.concatenate([-sin, sin], axis=1)


def _pack_w_in(w_in):
    d = w_in.shape[0]
    a = 3 * N_HEADS_A * HEAD_DIM_A + 2 * N_HEADS_B * DK_B + 2 * N_HEADS_B * DV_B
    ng = 2 * N_HEADS_B
    w_main = jnp.concatenate([w_in[:, :a], w_in[:, a + ng:]], axis=1).astype(BF16)
    w_gate = jnp.zeros((d, LANES), BF16).at[:, :ng].set(w_in[:, a:a + ng].astype(BF16))
    return w_main, w_gate


def _layer(x, c_mod, pos, w, *, cache=None, sample=None):
    (g_mix, w_main, w_gate, b_gate, w_out, g_ffn, w_up, w_conv, b_conv, w_down, g_fin) = w
    b, t, d = x.shape
    n = b * t
    wa = N_HEADS_A * HEAD_DIM_A
    dff = w_down.shape[0]
    hb = N_HEADS_B
    sh_m, sc_m, gt_m, sh_f, sc_f, gt_f = jnp.split(c_mod, 6, axis=-1)
    cos2, sin2 = _rope_tables(pos)
    x2d = x.reshape(n, d)

    if sample is None:
        per_seq = lambda a: a.reshape(b, 1, d)
        q, kf, kb16, vf, vb16, qkb, vbb, sga, sgob, gcol, grow, kmean = _inproj(
            x2d, g_mix, per_seq(sh_m), per_seq(sc_m), cos2, sin2, w_main, w_gate, b_gate,
            rows_per_mod=t, rows_per_pos=t)
        kmean = jnp.pad(kmean.reshape(b, t // MOBA_BLOCK, wa), ((0, 0), (0, LANES - t // MOBA_BLOCK), (0, 0)))
        ya, extra = _moba_prompt(q.reshape(b, t, wa), kb16.reshape(b, t, wa), vb16.reshape(b, t, wa), kmean, *cache)
        ya = ya.reshape(n, wa)
        chunk = MLSTM_CHUNK
        nc = t // chunk
        grow3 = grow.reshape(2 * hb, b * nc, chunk).transpose(1, 0, 2)
        c0 = jnp.zeros((b, hb, DV_B, DK_B), F32)
        n0 = jnp.zeros((b, hb, 1, DK_B), F32)
        m0 = jnp.zeros((b, hb, 1, LANES), F32)
        hbo, c_new, n_new, m_new = _mlstm(qkb.reshape(b, t, -1), vbb.reshape(b, t, -1), gcol.reshape(b, t, LANES),
                                          grow3, c0, n0, m0, chunk=chunk, valid_rows=chunk)
        y, tail = _ffn(x2d, ya, hbo.reshape(n, -1), sga, sgob, per_seq(gt_m), per_seq(sh_f), per_seq(sc_f),
                       per_seq(gt_f), g_ffn, g_fin, w_out, w_up, w_conv, b_conv, w_down, batch=b)
        conv_new = tail[:, SUBLANES - (CONV_W - 1):, :]
    else:
        cache_k, cache_v, page_table, kmean, c_st, n_st, m_st, conv0 = sample
        extra = None
        assert n == ROW_TILE and t >= CONV_W - 1
        past = page_table.shape[1] * cache_k.shape[1]
        assert past % MOBA_BLOCK == 0 and t <= MOBA_BLOCK
        per_row = lambda a: jnp.repeat(a, t, axis=0)
        cos2, sin2 = jnp.tile(cos2, (b, 1)), jnp.tile(sin2, (b, 1))
        q, kf, kb16, vf, vb16, qkb, vbb, sga, sgob, gcol, grow, _ = _inproj(
            x2d, g_mix, per_row(sh_m), per_row(sc_m), cos2, sin2, w_main, w_gate, b_gate,
            rows_per_mod=0, rows_per_pos=n)
        qf = q.astype(F32)
        sel = _select_blocks(qf.reshape(b, t, wa), kmean)[..., :MOBA_TOPK]
        ya = _moba_sample(qf, kf, vf, cache_k, cache_v, page_table, sel).astype(BF16)
        chunk = SAMPLE_CHUNK
        pad = lambda a: jnp.pad(a.reshape(b, t, -1), ((0, 0), (0, chunk - t), (0, 0)))
        grow3 = jnp.pad(grow.reshape(2 * hb, b, t).transpose(1, 0, 2), ((0, 0), (0, 0), (0, chunk - t)))
        m0 = jnp.broadcast_to(m_st[:, :, None, None], (b, hb, 1, LANES))
        hbo, c_new, n_new, m_new = _mlstm(pad(qkb), pad(vbb), pad(gcol), grow3, c_st, n_st[:, :, None, :], m0,
                                          chunk=chunk, valid_rows=t)
        hbo = hbo[:, :t].reshape(n, -1)
        zero = jnp.zeros((b, t - 2, dff), F32)
        fix1 = jnp.concatenate([conv0[:, 1:2], zero, zero[:, :1]], axis=1).reshape(n, dff)
        fix2 = jnp.concatenate([conv0, zero], axis=1).reshape(n, dff)
        y, a_full = _ffn(x2d, ya, hbo, sga, sgob, per_row(gt_m), per_row(sh_f), per_row(sc_f), per_row(gt_f),
                         g_ffn, g_fin, w_out, w_up, w_conv, b_conv, w_down, batch=b, seq_rows=t,
                         fix1=fix1, fix2=fix2)
        conv_new = a_full.reshape(b, t, dff)[:, t - (CONV_W - 1):]
    return (y.reshape(b, t, d), kf.reshape(b, t, N_HEADS_A, HEAD_DIM_A), vf.reshape(b, t, N_HEADS_A, HEAD_DIM_A),
            c_new, n_new[:, :, 0, :], m_new[:, :, 0, 0], conv_new, extra)


def kernel(x_prompt, x_sample, cache_k, cache_v, page_table, state_C, state_n, state_m, state_conv,
           c_prompt, c_sample, w_ada, b_ada, g_norm_mix, w_in, b_gates, w_out, g_norm_ffn,
           w_up, w_conv, b_conv, w_down, g_norm_final):
    depth = w_in.shape[0]
    assert depth == 1
    bp, tp, d = x_prompt.shape
    bs, ts, _ = x_sample.shape
    past = page_table.shape[1] * cache_k.shape[2]
    dl = 0
    mod = _ada(jnp.concatenate([c_prompt, c_sample], axis=0), w_ada[dl], b_ada[dl])
    w_main, w_gate = _pack_w_in(w_in[dl])
    b_gate = jnp.zeros((1, LANES), F32).at[0, :2 * N_HEADS_B].set(b_gates[dl])
    w = (g_norm_mix[dl].reshape(1, d), w_main, w_gate, b_gate, w_out[dl].astype(BF16),
         g_norm_ffn[dl].reshape(1, d), w_up[dl].astype(BF16), w_conv[dl], b_conv[dl].reshape(1, -1),
         w_down[dl].astype(BF16), g_norm_final.reshape(1, d))
    yp, kp, vp, cp, np_, mp, cvp, cache_kmean = _layer(
        x_prompt, mod[:bp], jnp.arange(tp, dtype=jnp.int32), w, cache=(cache_k[dl], page_table))
    ys, ks, vs, cs, ns, ms, cvs, _ = _layer(
        x_sample, mod[bp:], past + jnp.arange(ts, dtype=jnp.int32), w,
        sample=(cache_k[dl], cache_v[dl], page_table, cache_kmean,
                state_C[dl], state_n[dl], state_m[dl], state_conv[dl]))
    st = lambda a: a[None]
    return (yp, ys, st(kp), st(vp), st(ks), st(vs), st(cp), st(np_), st(mp), st(cs), st(ns), st(ms),
            st(cvp), st(cvs))
```

```python
import functools
import math

import jax
import jax.numpy as jnp
from jax import lax
from jax.experimental import pallas as pl
from jax.experimental.pallas import tpu as pltpu

F32 = jnp.float32
BF16 = jnp.bfloat16

N_HEADS_A = 8
HEAD_DIM_A = 128
MOBA_BLOCK = 256
MOBA_TOPK = 3
ROPE_THETA = 10000.0
N_HEADS_B = 4
DK_B = 128
DV_B = 256
CONV_W = 3
RMS_EPS = 1e-6
MASKED = -1e30
LANES = 128
SUBLANES = 8
ROW_TILE = 256
MLSTM_CHUNK = 256
SAMPLE_CHUNK = 128
FFN_CHUNK = 768
VMEM_LIMIT = 56 * 1024 * 1024


def _resident(shape):
    nd = len(shape)
    return pl.BlockSpec(shape, lambda *_: (0,) * nd, pipeline_mode=pl.Buffered(1))


def _params(sem):
    return pltpu.CompilerParams(dimension_semantics=sem, vmem_limit_bytes=VMEM_LIMIT)


def _split_bf16(x, pieces):
    out = []
    r = x
    for _ in range(pieces):
        p = r.astype(BF16)
        out.append(p)
        r = r - p.astype(F32)
    return out


def _dot_nt(a, b):
    return lax.dot_general(a, b, (((1,), (1,)), ((), ())), preferred_element_type=F32)


def _dot_tn(a, b):
    return lax.dot_general(a, b, (((0,), (0,)), ((), ())), preferred_element_type=F32)


def _dot(a, b):
    return jnp.dot(a, b, preferred_element_type=F32)


def _ada_kernel(c_ref, w_ref, b_ref, o_ref):
    c = c_ref[...]
    s = (c * jax.nn.sigmoid(c)).astype(BF16)
    o_ref[...] = _dot(s, w_ref[...].astype(BF16)) + b_ref[...]


def _ada(c, w_ada, b_ada):
    n, d = c.shape
    d_out = w_ada.shape[1]
    tn = d
    return pl.pallas_call(
        _ada_kernel,
        out_shape=jax.ShapeDtypeStruct((n, d_out), F32),
        grid=(d_out // tn,),
        in_specs=[pl.BlockSpec((n, d), lambda j: (0, 0)),
                  pl.BlockSpec((d, tn), lambda j: (0, j)),
                  pl.BlockSpec((1, tn), lambda j: (0, j))],
        out_specs=pl.BlockSpec((n, tn), lambda j: (0, j)),
        compiler_params=_params(("arbitrary",)),
        name="ada_mod",
    )(c, w_ada, b_ada.reshape(1, d_out))


def _inproj_kernel(x_ref, g_ref, sh_ref, sc_ref, cos_ref, sin_ref, w_ref, wg_ref, bg_ref,
                   q_ref, kf_ref, kb_ref, vf_ref, vb_ref, qkb_ref, vbb_ref, sga_ref, sgob_ref,
                   gc_ref, gr_ref, km_ref):
    tm, d = x_ref.shape
    wa = N_HEADS_A * HEAD_DIM_A
    x = x_ref[...]
    ms = jnp.mean(x * x, axis=-1, keepdims=True)
    xn = x * lax.rsqrt(ms + RMS_EPS) * g_ref[...]
    h = (xn * (1.0 + sc_ref[...]) + sh_ref[...]).astype(BF16)
    cos = cos_ref[...]
    sin = sin_ref[...]

    def proj(c):
        return _dot(h, w_ref[:, c * wa:(c + 1) * wa])

    def rope(a, hh):
        xh = a[:, hh * HEAD_DIM_A:(hh + 1) * HEAD_DIM_A]
        return xh * cos + pltpu.roll(xh, HEAD_DIM_A // 2, axis=1) * sin

    qscale = HEAD_DIM_A ** -0.5 * math.log2(math.e)
    a = proj(0)
    for hh in range(N_HEADS_A):
        sl = slice(hh * HEAD_DIM_A, (hh + 1) * HEAD_DIM_A)
        q_ref[:, sl] = (rope(a, hh) * qscale).astype(BF16)
    a = proj(1)
    for hh in range(N_HEADS_A):
        sl = slice(hh * HEAD_DIM_A, (hh + 1) * HEAD_DIM_A)
        kr = rope(a, hh)
        kf_ref[:, sl] = kr
        kb_ref[:, sl] = kr.astype(BF16)
        for g in range(tm // MOBA_BLOCK):
            km_ref[g, :, sl] = jnp.mean(kr[g * MOBA_BLOCK:(g + 1) * MOBA_BLOCK], axis=0, keepdims=True)
    a = proj(2)
    vf_ref[...] = a
    vb_ref[...] = a.astype(BF16)
    a = proj(3)
    wqk = N_HEADS_B * DK_B
    qkb_ref[:, :wqk] = a[:, :wqk].astype(BF16)
    qkb_ref[:, wqk:] = (a[:, wqk:] * (DK_B ** -0.5)).astype(BF16)
    vbb_ref[...] = proj(4).astype(BF16)
    so = jax.nn.sigmoid(proj(5))
    sga_ref[...] = jax.nn.sigmoid(proj(6)).astype(BF16)
    sgob_ref[...] = (jax.nn.sigmoid(proj(7)) * so).astype(BF16)
    g = _dot(h, wg_ref[...]) + bg_ref[...]
    lane = lax.broadcasted_iota(jnp.int32, g.shape, 1)
    logsig = jnp.minimum(g, 0.0) - jnp.log(1.0 + jnp.exp(-jnp.abs(g)))
    g = jnp.where(lane >= N_HEADS_B, logsig, g)
    gc_ref[...] = g
    gr_ref[...] = g.T[:2 * N_HEADS_B, :]


def _inproj(x, g_mix, shift, scale, cos2, sin2, w_main, w_gate, b_gate, *, rows_per_mod, rows_per_pos):
    n, d = x.shape
    tm = ROW_TILE
    assert n % tm == 0 and tm % MOBA_BLOCK == 0
    wa = N_HEADS_A * HEAD_DIM_A
    if rows_per_mod:
        assert rows_per_mod % tm == 0
        mod_spec = pl.BlockSpec((None, 1, d), lambda i: (i // (rows_per_mod // tm), 0, 0))
    else:
        mod_spec = pl.BlockSpec((tm, d), lambda i: (i, 0))
    assert rows_per_pos % tm == 0
    pos_spec = pl.BlockSpec((tm, HEAD_DIM_A), lambda i: (i % (rows_per_pos // tm), 0))
    row = lambda w: pl.BlockSpec((tm, w), lambda i: (i, 0))
    nblk = tm // MOBA_BLOCK
    out_shape = (
        jax.ShapeDtypeStruct((n, wa), BF16),
        jax.ShapeDtypeStruct((n, wa), F32),
        jax.ShapeDtypeStruct((n, wa), BF16),
        jax.ShapeDtypeStruct((n, wa), F32),
        jax.ShapeDtypeStruct((n, wa), BF16),
        jax.ShapeDtypeStruct((n, 2 * N_HEADS_B * DK_B), BF16),
        jax.ShapeDtypeStruct((n, N_HEADS_B * DV_B), BF16),
        jax.ShapeDtypeStruct((n, wa), BF16),
        jax.ShapeDtypeStruct((n, N_HEADS_B * DV_B), BF16),
        jax.ShapeDtypeStruct((n, LANES), F32),
        jax.ShapeDtypeStruct((2 * N_HEADS_B, n), F32),
        jax.ShapeDtypeStruct((n // MOBA_BLOCK, 1, wa), F32),
    )
    out_specs = (row(wa), row(wa), row(wa), row(wa), row(wa), row(2 * N_HEADS_B * DK_B), row(N_HEADS_B * DV_B),
                 row(wa), row(N_HEADS_B * DV_B), row(LANES),
                 pl.BlockSpec((2 * N_HEADS_B, tm), lambda i: (0, i)),
                 pl.BlockSpec((nblk, 1, wa), lambda i: (i, 0, 0)))
    return pl.pallas_call(
        _inproj_kernel,
        out_shape=out_shape,
        grid=(n // tm,),
        in_specs=[row(d), _resident((1, d)), mod_spec, mod_spec, pos_spec, pos_spec,
                  _resident(w_main.shape), _resident(w_gate.shape), _resident((1, LANES))],
        out_specs=out_specs,
        compiler_params=_params(("parallel",)),
        name="in_proj",
    )(x, g_mix, shift, scale, cos2, sin2, w_main, w_gate, b_gate)


def _topk_mask(gate, valid, axis):
    blk = lax.broadcasted_iota(jnp.int32, gate.shape, axis).astype(F32)
    g = jnp.where(valid, gate, -jnp.inf)
    sel = jnp.zeros(gate.shape, F32)
    for _ in range(MOBA_TOPK):
        mx = jnp.max(g, axis=axis, keepdims=True)
        idx = jnp.min(jnp.where(g == mx, blk, float(gate.shape[axis])), axis=axis, keepdims=True)
        pick = blk == idx
        sel = jnp.where(pick, jnp.where(valid, 1.0, 0.0), sel)
        g = jnp.where(pick, -jnp.inf, g)
    return sel


def _gate_scores(q, kmean):
    hi, lo = _split_bf16(kmean, 2)
    return _dot_nt(q, hi) + _dot_nt(q, lo)


def _gate_scores_t(q, kmean):
    hi, lo = _split_bf16(kmean, 2)
    return _dot_nt(hi, q) + _dot_nt(lo, q)


def _moba_prompt_kernel(pt_ref, q_ref, k_ref, v_ref, km_ref, ck_ref, o_ref, cm_ref,
                        kaug, vext, qa, s0, s1, mx0, mx1, m_ref, acc_ref, pbuf, psem, *, pages_per_step, tk):
    i = pl.program_id(2)
    tq, hd = q_ref.shape
    t = k_ref.shape[0]
    n_own = tq // tk
    step = (pl.program_id(0) * pl.num_programs(1) + pl.program_id(1)) * pl.num_programs(2) + i
    n_steps = pl.num_programs(0) * pl.num_programs(1) * pl.num_programs(2)

    def page_copy(stp, p):
        half = stp % 2
        return pltpu.make_async_copy(ck_ref.at[pt_ref[stp * pages_per_step + p]],
                                     pbuf.at[half * pages_per_step + p], psem.at[half])

    @pl.when(i == 0)
    def _():
        vext[:, :hd] = v_ref[...]
        vext[:, hd:] = jnp.ones((t, hd), BF16)
        rowi = lax.broadcasted_iota(jnp.int32, (hd, tq), 0)
        coli = lax.broadcasted_iota(jnp.int32, (hd, tq), 1)
        for j in range(t // tq):
            kaug[:hd, j * tq:(j + 1) * tq] = k_ref[j * tq:(j + 1) * tq, :].T
            blk_of_key = j * (tq // MOBA_BLOCK) + coli // MOBA_BLOCK
            kaug[hd:, j * tq:(j + 1) * tq] = jnp.where(rowi == blk_of_key, 1.0, 0.0).astype(BF16)

    @pl.when(step == 0)
    def _():
        for p in range(pages_per_step):
            page_copy(step, p).start()

    @pl.when(step + 1 < n_steps)
    def _():
        for p in range(pages_per_step):
            page_copy(step + 1, p).start()

    for p in range(pages_per_step):
        page_copy(step, p).wait()
    per_blk = MOBA_BLOCK // pbuf.shape[1]
    for r in range(pages_per_step // per_blk):
        base = (step % 2) * pages_per_step + r * per_blk
        tot = jnp.sum(pbuf[base], axis=0)
        for p in range(1, per_blk):
            tot = tot + jnp.sum(pbuf[base + p], axis=0)
        cm_ref[r] = tot * (1.0 / MOBA_BLOCK)

    q = q_ref[...]
    nbp = -(-(t // MOBA_BLOCK) // SUBLANES) * SUBLANES
    gate = _gate_scores_t(q, km_ref[:nbp, :])
    blk = lax.broadcasted_iota(jnp.int32, gate.shape, 0)
    own = i * (tq // MOBA_BLOCK) + lax.broadcasted_iota(jnp.int32, gate.shape, 1) // MOBA_BLOCK
    sel = _topk_mask(gate, blk < own, 0)
    bias = jnp.where(blk == own, 0.0, jnp.where(sel > 0.5, 0.0, MASKED))
    bias = jnp.concatenate([bias, jnp.zeros((LANES - nbp, tq), F32)], axis=0)
    qa[:, :hd] = q
    qa[:, hd:] = bias.T.astype(BF16)
    m_ref[...] = jnp.full(m_ref.shape, MASKED, F32)
    acc_ref[...] = jnp.zeros(acc_ref.shape, F32)

    n_lt = tk // LANES

    def lane_tile_max(s):
        pm = s[:, :LANES]
        for j in range(1, n_lt):
            pm = jnp.maximum(pm, s[:, j * LANES:(j + 1) * LANES])
        return pm

    def scores_to(slot, e, row0=0):
        s_ref, mx_ref = slot
        st = pl.multiple_of(e * tk, tk)
        s = _dot(qa[row0:, :], kaug[:, pl.ds(st, tk)])
        s_ref[row0:, :] = s
        mx_ref[row0:, :] = lane_tile_max(s)

    def absorb(slot, e, row0=0):
        s_ref, mx_ref = slot
        st = pl.multiple_of(e * tk, tk)
        m = m_ref[row0:, :]
        m_new = jnp.maximum(m, jnp.broadcast_to(jnp.max(mx_ref[row0:, :], axis=1, keepdims=True), m.shape))
        p = jnp.concatenate([jnp.exp2(s_ref[row0:, j * LANES:(j + 1) * LANES] - m_new).astype(BF16)
                             for j in range(n_lt)], axis=1)
        alpha = jnp.exp2(m - m_new)
        pv = _dot(p, vext[pl.ds(st, tk), :])
        acc_ref[row0:, :hd] = alpha * acc_ref[row0:, :hd] + pv[:, :hd]
        acc_ref[row0:, hd:] = alpha * acc_ref[row0:, hd:] + pv[:, hd:]
        m_ref[row0:, :] = m_new

    def absorb_own(slot, j):
        s_ref, mx_ref = slot
        rows = slice(j * tk, (j + 1) * tk)
        r = lax.broadcasted_iota(jnp.int32, (tk, tk), 0)
        c = lax.broadcasted_iota(jnp.int32, (tk, tk), 1)
        s = jnp.where(c <= r, s_ref[rows, :], MASKED)
        s_ref[rows, :] = s
        mx_ref[rows, :] = lane_tile_max(s)
        absorb(slot, i * n_own + j, row0=j * tk)

    slot0, slot1 = (s0, mx0), (s1, mx1)
    scores_to(slot0, 0)

    def pair(p, carry):
        e = 2 * p
        scores_to(slot1, e + 1)
        absorb(slot0, e)
        scores_to(slot0, e + 2)
        absorb(slot1, e + 1)
        return carry

    lax.fori_loop(0, i * (n_own // 2), pair, 0)

    slots = (slot0, slot1)
    for j in range(n_own):
        if j + 1 < n_own:
            scores_to(slots[(j + 1) % 2], i * n_own + j + 1, row0=(j + 1) * tk)
        absorb_own(slots[j % 2], j)
    o_ref[...] = (acc_ref[:, :hd] / acc_ref[:, hd:]).astype(o_ref.dtype)


def _moba_prompt(q, k, v, kmean, cache_k, page_table, *, blocks_per_key_tile=2, key_tiles_per_query_tile=2):
    b, t, w = q.shape
    tk = blocks_per_key_tile * MOBA_BLOCK
    tq = key_tiles_per_query_tile * tk
    hd = HEAD_DIM_A
    nh = w // hd
    assert key_tiles_per_query_tile % 2 == 0
    assert t % tq == 0 and t // MOBA_BLOCK <= LANES and kmean.shape[1] == LANES
    n_steps = b * nh * (t // tq)
    bs, n_pages = page_table.shape
    page = cache_k.shape[1]
    per_blk = MOBA_BLOCK // page
    pages_per_step = bs * n_pages // n_steps
    assert pages_per_step * n_steps == bs * n_pages and pages_per_step % per_blk == 0 and MOBA_BLOCK % page == 0
    blk_per_step = pages_per_step // per_blk

    def step_of(bb, hh, i):
        return (bb * nh + hh) * (t // tq) + i

    ya, cmean = pl.pallas_call(
        functools.partial(_moba_prompt_kernel, pages_per_step=pages_per_step, tk=tk),
        out_shape=(jax.ShapeDtypeStruct((b, t, w), BF16),
                   jax.ShapeDtypeStruct((n_steps, blk_per_step) + cache_k.shape[2:], F32)),
        grid_spec=pltpu.PrefetchScalarGridSpec(
            num_scalar_prefetch=1,
            grid=(b, nh, t // tq),
            in_specs=[pl.BlockSpec((None, tq, hd), lambda bb, hh, i, pt: (bb, i, hh)),
                      pl.BlockSpec((None, t, hd), lambda bb, hh, i, pt: (bb, 0, hh)),
                      pl.BlockSpec((None, t, hd), lambda bb, hh, i, pt: (bb, 0, hh)),
                      pl.BlockSpec((None, LANES, hd), lambda bb, hh, i, pt: (bb, 0, hh)),
                      pl.BlockSpec(memory_space=pl.ANY)],
            out_specs=(pl.BlockSpec((None, tq, hd), lambda bb, hh, i, pt: (bb, i, hh)),
                       pl.BlockSpec((None, blk_per_step) + cache_k.shape[2:],
                                    lambda bb, hh, i, pt: (step_of(bb, hh, i), 0, 0, 0))),
            scratch_shapes=[pltpu.VMEM((2 * hd, t), BF16), pltpu.VMEM((t, 2 * hd), BF16),
                            pltpu.VMEM((tq, 2 * hd), BF16),
                            pltpu.VMEM((tq, tk), F32), pltpu.VMEM((tq, tk), F32),
                            pltpu.VMEM((tq, LANES), F32), pltpu.VMEM((tq, LANES), F32),
                            pltpu.VMEM((tq, LANES), F32), pltpu.VMEM((tq, 2 * hd), F32),
                            pltpu.VMEM((2 * pages_per_step,) + cache_k.shape[1:], F32),
                            pltpu.SemaphoreType.DMA((2,))]),
        compiler_params=_params(("arbitrary", "arbitrary", "arbitrary")),
        name="moba_prompt",
    )(page_table.reshape(-1), q, k, v, kmean, cache_k)
    return ya, cmean.reshape((bs, n_pages // per_blk) + cache_k.shape[2:])


def _mlstm_kernel(qk_ref, v_ref, gc_ref, gr_ref, c0_ref, n0_ref, m0_ref,
                  h_ref, c_out, n_out, m_out, c_s, n_s, m_s, *, valid_rows):
    ci = pl.program_id(1)
    L = qk_ref.shape[0]
    wqk = N_HEADS_B * DK_B

    @pl.when(ci == 0)
    def _():
        c_s[...] = c0_ref[...]
        n_s[...] = n0_ref[...]
        m_s[...] = m0_ref[...]

    gc = gc_ref[...]
    gr = gr_ref[...]
    if valid_rows < L:
        rr = lax.broadcasted_iota(jnp.int32, gc.shape, 0)
        ln = lax.broadcasted_iota(jnp.int32, gc.shape, 1)
        gc = jnp.where(rr < valid_rows, gc, jnp.where(ln < N_HEADS_B, MASKED, 0.0))
        cc = lax.broadcasted_iota(jnp.int32, gr.shape, 1)
        rw = lax.broadcasted_iota(jnp.int32, gr.shape, 0)
        gr = jnp.where(cc < valid_rows, gr, jnp.where(rw < N_HEADS_B, MASKED, 0.0))

    r = lax.broadcasted_iota(jnp.int32, (L, L), 0)
    c = lax.broadcasted_iota(jnp.int32, (L, L), 1)
    causal = c <= r
    tri = jnp.where(causal, 1.0, 0.0).astype(BF16)
    triu = jnp.where(r <= c, 1.0, 0.0).astype(BF16)
    bc_all = sum(_dot(tri, p) for p in _split_bf16(gc, 3))
    br_all = sum(_dot(p, triu) for p in _split_bf16(gr, 3))

    def rep(x, width):
        return jnp.concatenate([x] * (width // LANES), axis=1)

    ones = jnp.ones((L, LANES), BF16)
    for hh in range(N_HEADS_B):
        q = qk_ref[:, hh * DK_B:(hh + 1) * DK_B]
        k = qk_ref[:, wqk + hh * DK_B:wqk + (hh + 1) * DK_B]
        v1 = jnp.concatenate([v_ref[:, hh * DV_B:(hh + 1) * DV_B], ones], axis=1)
        i_col = jnp.broadcast_to(gc[:, hh:hh + 1], (L, LANES))
        b_col = jnp.broadcast_to(bc_all[:, N_HEADS_B + hh:N_HEADS_B + hh + 1], (L, LANES))
        i_row = gr[hh:hh + 1, :]
        b_row = br_all[N_HEADS_B + hh:N_HEADS_B + hh + 1, :]
        C = c_s[hh]
        n = n_s[hh]
        m_prev = m_s[hh]

        dmat = jnp.where(causal, rep(b_col, L) - b_row + i_row, MASKED)
        inter = b_col + m_prev
        m_t = jnp.maximum(inter, jnp.broadcast_to(jnp.max(dmat, axis=1, keepdims=True), (L, LANES)))
        w = jnp.exp(dmat - rep(m_t, L))
        a_inter = jnp.exp(inter - m_t)
        qk = (_dot_nt(q, k) * w).astype(BF16)
        cn = jnp.concatenate([C, jnp.broadcast_to(n, (LANES, DK_B))], axis=0).astype(BF16)
        qc = _dot_nt(q, cn)
        qv = _dot(qk, v1)
        num = rep(a_inter, DV_B) * qc[:, :DV_B] + qv[:, :DV_B]
        den = a_inter * qc[:, DV_B:] + qv[:, DV_B:]
        inv = 1.0 / jnp.maximum(jnp.abs(den), jnp.exp(-m_t))
        h_ref[:, hh * DV_B:(hh + 1) * DV_B] = (num * rep(inv, DV_B)).astype(h_ref.dtype)

        b_last = b_col[L - 1:L, :]
        g_col = b_last - b_col + i_col
        m_new = jnp.maximum(b_last + m_prev, jnp.max(g_col, axis=0, keepdims=True))
        decay = jnp.exp(b_last + m_prev - m_new)
        kw = k.astype(F32) * jnp.exp(g_col - m_new)
        c_s[hh] = decay * C + _dot_tn(v1[:, :DV_B], kw.astype(BF16))
        n_s[hh] = decay * n + jnp.sum(kw, axis=0, keepdims=True)
        m_s[hh] = m_new

    @pl.when(ci == pl.num_programs(1) - 1)
    def _():
        c_out[...] = c_s[...]
        n_out[...] = n_s[...]
        m_out[...] = m_s[...]


def _mlstm(qkb, vb, gates_col, gates_row, c0, n0, m0, *, chunk, valid_rows):
    b, t, _ = qkb.shape
    nc = t // chunk
    hb = N_HEADS_B
    st4 = lambda shp: pl.BlockSpec((None,) + shp, lambda bb, ci: (bb, 0, 0, 0))
    kern = functools.partial(_mlstm_kernel, valid_rows=valid_rows)
    return pl.pallas_call(
        kern,
        out_shape=(jax.ShapeDtypeStruct((b, t, hb * DV_B), BF16),
                   jax.ShapeDtypeStruct(c0.shape, F32),
                   jax.ShapeDtypeStruct(n0.shape, F32),
                   jax.ShapeDtypeStruct(m0.shape, F32)),
        grid=(b, nc),
        in_specs=[pl.BlockSpec((None, chunk, qkb.shape[2]), lambda bb, ci: (bb, ci, 0)),
                  pl.BlockSpec((None, chunk, vb.shape[2]), lambda bb, ci: (bb, ci, 0)),
                  pl.BlockSpec((None, chunk, LANES), lambda bb, ci: (bb, ci, 0)),
                  pl.BlockSpec((None, 2 * hb, chunk), lambda bb, ci: (bb * nc + ci, 0, 0)),
                  st4((hb, DV_B, DK_B)), st4((hb, 1, DK_B)), st4((hb, 1, LANES))],
        out_specs=(pl.BlockSpec((None, chunk, hb * DV_B), lambda bb, ci: (bb, ci, 0)),
                   st4((hb, DV_B, DK_B)), st4((hb, 1, DK_B)), st4((hb, 1, LANES))),
        scratch_shapes=[pltpu.VMEM((hb, DV_B, DK_B), F32),
                        pltpu.VMEM((hb, 1, DK_B), F32),
                        pltpu.VMEM((hb, 1, LANES), F32)],
        compiler_params=_params(("parallel", "arbitrary")),
        name="mlstm",
    )(qkb, vb, gates_col, gates_row, c0, n0, m0)


def _ffn_kernel(x_ref, ya_ref, hb_ref, sga_ref, sgob_ref, gtm_ref, shf_ref, scf_ref, gtf_ref,
                gffn_ref, gfin_ref, wout_ref, wup_ref, wconv_ref, bconv_ref, wdown_ref, *rest, seq_rows):
    if seq_rows:
        fix1_ref, fix2_ref, y_ref, a_ref = rest
    else:
        y_ref, tail_ref, carry = rest

        @pl.when(pl.program_id(1) == 0)
        def _():
            carry[...] = jnp.zeros_like(carry)
    tm = x_ref.shape[0]
    dff = wdown_ref.shape[0]
    merged = (sga_ref[...].astype(F32) * ya_ref[...].astype(F32)
              + sgob_ref[...].astype(F32) * hb_ref[...].astype(F32))
    x1 = x_ref[...] + gtm_ref[...] * _dot(merged.astype(BF16), wout_ref[...])
    ms = jnp.mean(x1 * x1, axis=-1, keepdims=True)
    xn = x1 * lax.rsqrt(ms + RMS_EPS) * gffn_ref[...]
    h = (xn * (1.0 + scf_ref[...]) + shf_ref[...]).astype(BF16)

    edges = list(range(0, dff, FFN_CHUNK)) + [dff]
    chunks = list(zip(edges[:-1], edges[1:]))

    def up(c0, c1):
        return _dot(h, wup_ref[:, c0:c1]), _dot(h, wup_ref[:, dff + c0:dff + c1])

    nxt = up(*chunks[0])
    f = None
    for ci, (c0, c1) in enumerate(chunks):
        a, gb = nxt
        if ci + 1 < len(chunks):
            nxt = up(*chunks[ci + 1])
        row = lax.broadcasted_iota(jnp.int32, a.shape, 0)
        r1 = pltpu.roll(a, 1, axis=0)
        r2 = pltpu.roll(a, 2, axis=0)
        if seq_rows:
            assert seq_rows & (seq_rows - 1) == 0
            t = row & (seq_rows - 1)
            a1 = jnp.where(t >= 1, r1, fix1_ref[:, c0:c1])
            a2 = jnp.where(t >= 2, r2, fix2_ref[:, c0:c1])
            a_ref[:, c0:c1] = a
        else:
            prev1 = carry[SUBLANES - 1:SUBLANES, c0:c1]
            prev2 = carry[SUBLANES - 2:SUBLANES - 1, c0:c1]
            a1 = jnp.where(row == 0, prev1, r1)
            a2 = jnp.where(row == 0, prev2, jnp.where(row == 1, prev1, r2))
            tail = a[tm - SUBLANES:, :]
            carry[:, c0:c1] = tail
            tail_ref[:, c0:c1] = tail
        ac = (bconv_ref[:, c0:c1] + a2 * wconv_ref[0:1, c0:c1] + a1 * wconv_ref[1:2, c0:c1]
              + a * wconv_ref[2:3, c0:c1])
        gelu = 0.5 * ac * (1.0 + jnp.tanh(math.sqrt(2.0 / math.pi) * (ac + 0.044715 * (ac * ac * ac))))
        part = _dot((gelu * gb).astype(BF16), wdown_ref[c0:c1, :])
        f = part if f is None else f + part
    x2 = x1 + gtf_ref[...] * f
    ms2 = jnp.mean(x2 * x2, axis=-1, keepdims=True)
    y_ref[...] = x2 * lax.rsqrt(ms2 + RMS_EPS) * gfin_ref[...]


def _ffn(x, ya, hb, sga, sgob, gtm, shf, scf, gtf, g_ffn, g_fin, w_out, w_up, w_conv, b_conv, w_down,
         *, batch, seq_rows=0, fix1=None, fix2=None):
    n, d = x.shape
    dff = w_down.shape[0]
    tm = ROW_TILE
    if seq_rows:
        assert n == tm
        grid = (1, 1)
        mod_spec = pl.BlockSpec((tm, d), lambda bb, i: (0, 0))
        nt = 1
    else:
        nt = n // batch // tm
        grid = (batch, nt)
        mod_spec = pl.BlockSpec((None, 1, d), lambda bb, i: (bb, 0, 0))
    row = lambda w: pl.BlockSpec((tm, w), lambda bb, i: (bb * nt + i, 0))
    in_specs = [row(d), row(d), row(d), row(d), row(d), mod_spec, mod_spec, mod_spec, mod_spec,
                _resident((1, d)), _resident((1, d)), _resident(w_out.shape), _resident(w_up.shape),
                _resident(w_conv.shape), _resident((1, dff)), _resident(w_down.shape)]
    args = [x, ya, hb, sga, sgob, gtm, shf, scf, gtf, g_ffn, g_fin, w_out, w_up, w_conv, b_conv, w_down]
    if seq_rows:
        in_specs += [row(dff), row(dff)]
        args += [fix1, fix2]
        out_shape = (jax.ShapeDtypeStruct((n, d), F32), jax.ShapeDtypeStruct((n, dff), F32))
        out_specs = (row(d), row(dff))
        scratch = []
    else:
        out_shape = (jax.ShapeDtypeStruct((n, d), F32), jax.ShapeDtypeStruct((batch, SUBLANES, dff), F32))
        out_specs = (row(d), pl.BlockSpec((None, SUBLANES, dff), lambda bb, i: (bb, 0, 0)))
        scratch = [pltpu.VMEM((SUBLANES, dff), F32)]
    return pl.pallas_call(
        functools.partial(_ffn_kernel, seq_rows=seq_rows),
        out_shape=out_shape,
        grid=grid,
        in_specs=in_specs,
        out_specs=out_specs,
        scratch_shapes=scratch,
        compiler_params=_params(("parallel", "arbitrary")),
        name="merge_ffn",
    )(*args)


def _select_kernel(q_ref, km_ref, o_ref):
    n_blocks = km_ref.shape[0]
    ts = q_ref.shape[0]
    lane = lax.broadcasted_iota(jnp.int32, (ts, LANES), 1)
    for hh in range(N_HEADS_A):
        sl = slice(hh * HEAD_DIM_A, (hh + 1) * HEAD_DIM_A)
        gate = _gate_scores(q_ref[:, sl].astype(BF16), km_ref[:, hh, :])
        g = gate
        blk = lax.broadcasted_iota(jnp.int32, gate.shape, 1).astype(F32)
        out = jnp.zeros((ts, LANES), F32)
        for r in range(MOBA_TOPK):
            mx = jnp.max(g, axis=1, keepdims=True)
            idx = jnp.min(jnp.where(g == mx, blk, float(n_blocks)), axis=1, keepdims=True)
            out = jnp.where(lane == r, idx, out)
            g = jnp.where(blk == idx, -jnp.inf, g)
        o_ref[hh] = out.astype(jnp.int32)


def _select_blocks(q, kmean):
    b, ts, w = q.shape
    _, nb, nh, hd = kmean.shape
    return pl.pallas_call(
        _select_kernel,
        out_shape=jax.ShapeDtypeStruct((b, N_HEADS_A, ts, LANES), jnp.int32),
        grid=(b,),
        in_specs=[pl.BlockSpec((None, ts, w), lambda bb: (bb, 0, 0)),
                  pl.BlockSpec((None, nb, nh, hd), lambda bb: (bb, 0, 0, 0))],
        out_specs=pl.BlockSpec((None, N_HEADS_A, ts, LANES), lambda bb: (bb, 0, 0, 0)),
        compiler_params=_params(("parallel",)),
        name="select_blocks",
    )(q, kmean)


def _moba_sample_kernel(pg_ref, q_ref, kn_ref, vn_ref, ck_ref, cv_ref, o_ref, kbuf, vbuf, sem, *, per_blk):
    b = pl.program_id(0)
    hh = pl.program_id(1)
    ts = q_ref.shape[0]
    page = kbuf.shape[1]
    nsel = MOBA_TOPK * per_blk

    step = b * N_HEADS_A + hh
    nsteps = pl.num_programs(0) * N_HEADS_A
    nslots = ts * nsel

    def start_all(stp):
        head = stp % N_HEADS_A
        half = stp % 2
        for j in range(nslots):
            pg = pg_ref[stp * nslots + j]
            pltpu.make_async_copy(ck_ref.at[pg, :, head, :], kbuf.at[half * nslots + j], sem.at[0, half]).start()
            pltpu.make_async_copy(cv_ref.at[pg, :, head, :], vbuf.at[half * nslots + j], sem.at[1, half]).start()

    @pl.when(step == 0)
    def _():
        start_all(step)

    @pl.when(step + 1 < nsteps)
    def _():
        start_all(step + 1)

    base = (step % 2) * nslots
    for buf, kind in ((kbuf, 0), (vbuf, 1)):
        whole = buf.at[pl.ds(base, nslots)]
        pltpu.make_async_copy(whole, whole, sem.at[kind, step % 2]).wait()

    qf = q_ref[...]
    q = qf.astype(BF16)
    kn = kn_ref[...]
    vn = vn_ref[...]
    jrow = lax.broadcasted_iota(jnp.int32, (ts, LANES), 0)
    for t in range(ts):
        qrep = jnp.broadcast_to(q[t:t + 1, :], (LANES, q.shape[1]))
        kc = kbuf[pl.ds(base + t * nsel, nsel)].reshape(nsel * page, -1).astype(BF16)
        vc = vbuf[pl.ds(base + t * nsel, nsel)].reshape(nsel * page, -1)
        s = _dot_nt(kc, qrep)
        s_own = jnp.where(jrow <= t, jnp.sum(kn * qf[t:t + 1, :], axis=1, keepdims=True), MASKED)
        m = jnp.maximum(jnp.max(s, axis=0, keepdims=True), jnp.max(s_own, axis=0, keepdims=True))
        p = jnp.exp2(s - m)
        p_own = jnp.exp2(s_own - m)
        l = jnp.sum(p, axis=0, keepdims=True) + jnp.sum(p_own, axis=0, keepdims=True)
        o = jnp.sum(p * vc, axis=0, keepdims=True) + jnp.sum(p_own * vn, axis=0, keepdims=True)
        o_ref[t:t + 1, :] = o / l


def _moba_sample(q, k_new, v_new, cache_k, cache_v, page_table, sel):
    b, n_pages = page_table.shape
    ts = q.shape[0] // b
    _, page, nh, hd = cache_k.shape
    per_blk = MOBA_BLOCK // page
    nslots = ts * MOBA_TOPK * per_blk
    logical = sel[..., None] * per_blk + jnp.arange(per_blk, dtype=jnp.int32)
    pages = jnp.take_along_axis(page_table, logical.reshape(b, -1), axis=1)
    blk = pl.BlockSpec((ts, hd), lambda bb, hh, *_: (bb, hh))
    kern = functools.partial(_moba_sample_kernel, per_blk=per_blk)
    return pl.pallas_call(
        kern,
        out_shape=jax.ShapeDtypeStruct(q.shape, F32),
        grid_spec=pltpu.PrefetchScalarGridSpec(
            num_scalar_prefetch=1,
            grid=(b, nh),
            in_specs=[blk, blk, blk, pl.BlockSpec(memory_space=pl.ANY), pl.BlockSpec(memory_space=pl.ANY)],
            out_specs=blk,
            scratch_shapes=[pltpu.VMEM((2 * nslots, page, hd), F32), pltpu.VMEM((2 * nslots, page, hd), F32),
                            pltpu.SemaphoreType.DMA((2, 2))]),
        compiler_params=_params(("arbitrary", "arbitrary")),
        name="moba_sample",
    )(pages.reshape(-1), q, k_new, v_new, cache_k, cache_v)


def _rope_tables(pos):
    half = HEAD_DIM_A // 2
    inv = ROPE_THETA ** (-jnp.arange(half, dtype=F32) / half)
    ang = pos.astype(F32)[:, None] * inv[None, :]
    cos, sin = jnp.cos(ang), jnp.sin(ang)
    return jnp.concatenate([cos, cos], axis=1), jnp.concatenate([-sin, sin], axis=1)


def _pack_w_in(w_in):
    d = w_in.shape[0]
    a = 3 * N_HEADS_A * HEAD_DIM_A + 2 * N_HEADS_B * DK_B + 2 * N_HEADS_B * DV_B
    ng = 2 * N_HEADS_B
    w_main = jnp.concatenate([w_in[:, :a], w_in[:, a + ng:]], axis=1).astype(BF16)
    w_gate = jnp.zeros((d, LANES), BF16).at[:, :ng].set(w_in[:, a:a + ng].astype(BF16))
    return w_main, w_gate


def _layer(x, c_mod, pos, w, *, cache=None, sample=None):
    (g_mix, w_main, w_gate, b_gate, w_out, g_ffn, w_up, w_conv, b_conv, w_down, g_fin) = w
    b, t, d = x.shape
    n = b * t
    wa = N_HEADS_A * HEAD_DIM_A
    dff = w_down.shape[0]
    hb = N_HEADS_B
    sh_m, sc_m, gt_m, sh_f, sc_f, gt_f = jnp.split(c_mod, 6, axis=-1)
    cos2, sin2 = _rope_tables(pos)
    x2d = x.reshape(n, d)

    if sample is None:
        per_seq = lambda a: a.reshape(b, 1, d)
        q, kf, kb16, vf, vb16, qkb, vbb, sga, sgob, gcol, grow, kmean = _inproj(
            x2d, g_mix, per_seq(sh_m), per_seq(sc_m), cos2, sin2, w_main, w_gate, b_gate,
            rows_per_mod=t, rows_per_pos=t)
        kmean = jnp.pad(kmean.reshape(b, t // MOBA_BLOCK, wa), ((0, 0), (0, LANES - t // MOBA_BLOCK), (0, 0)))
        ya, extra = _moba_prompt(q.reshape(b, t, wa), kb16.reshape(b, t, wa), vb16.reshape(b, t, wa), kmean, *cache)
        ya = ya.reshape(n, wa)
        chunk = MLSTM_CHUNK
        nc = t // chunk
        grow3 = grow.reshape(2 * hb, b * nc, chunk).transpose(1, 0, 2)
        c0 = jnp.zeros((b, hb, DV_B, DK_B), F32)
        n0 = jnp.zeros((b, hb, 1, DK_B), F32)
        m0 = jnp.zeros((b, hb, 1, LANES), F32)
        hbo, c_new, n_new, m_new = _mlstm(qkb.reshape(b, t, -1), vbb.reshape(b, t, -1), gcol.reshape(b, t, LANES),
                                          grow3, c0, n0, m0, chunk=chunk, valid_rows=chunk)
        y, tail = _ffn(x2d, ya, hbo.reshape(n, -1), sga, sgob, per_seq(gt_m), per_seq(sh_f), per_seq(sc_f),
                       per_seq(gt_f), g_ffn, g_fin, w_out, w_up, w_conv, b_conv, w_down, batch=b)
        conv_new = tail[:, SUBLANES - (CONV_W - 1):, :]
    else:
        cache_k, cache_v, page_table, kmean, c_st, n_st, m_st, conv0 = sample
        extra = None
        assert n == ROW_TILE and t >= CONV_W - 1
        past = page_table.shape[1] * cache_k.shape[1]
        assert past % MOBA_BLOCK == 0 and t <= MOBA_BLOCK
        per_row = lambda a: jnp.repeat(a, t, axis=0)
        cos2, sin2 = jnp.tile(cos2, (b, 1)), jnp.tile(sin2, (b, 1))
        q, kf, kb16, vf, vb16, qkb, vbb, sga, sgob, gcol, grow, _ = _inproj(
            x2d, g_mix, per_row(sh_m), per_row(sc_m), cos2, sin2, w_main, w_gate, b_gate,
            rows_per_mod=0, rows_per_pos=n)
        qf = q.astype(F32)
        sel = _select_blocks(qf.reshape(b, t, wa), kmean)[..., :MOBA_TOPK]
        ya = _moba_sample(qf, kf, vf, cache_k, cache_v, page_table, sel).astype(BF16)
        chunk = SAMPLE_CHUNK
        pad = lambda a: jnp.pad(a.reshape(b, t, -1), ((0, 0), (0, chunk - t), (0, 0)))
        grow3 = jnp.pad(grow.reshape(2 * hb, b, t).transpose(1, 0, 2), ((0, 0), (0, 0), (0, chunk - t)))
        m0 = jnp.broadcast_to(m_st[:, :, None, None], (b, hb, 1, LANES))
        hbo, c_new, n_new, m_new = _mlstm(pad(qkb), pad(vbb), pad(gcol), grow3, c_st, n_st[:, :, None, :], m0,
                                          chunk=chunk, valid_rows=t)
        hbo = hbo[:, :t].reshape(n, -1)
        zero = jnp.zeros((b, t - 2, dff), F32)
        fix1 = jnp.concatenate([conv0[:, 1:2], zero, zero[:, :1]], axis=1).reshape(n, dff)
        fix2 = jnp.concatenate([conv0, zero], axis=1).reshape(n, dff)
        y, a_full = _ffn(x2d, ya, hbo, sga, sgob, per_row(gt_m), per_row(sh_f), per_row(sc_f), per_row(gt_f),
                         g_ffn, g_fin, w_out, w_up, w_conv, b_conv, w_down, batch=b, seq_rows=t,
                         fix1=fix1, fix2=fix2)
        conv_new = a_full.reshape(b, t, dff)[:, t - (CONV_W - 1):]
    return (y.reshape(b, t, d), kf.reshape(b, t, N_HEADS_A, HEAD_DIM_A), vf.reshape(b, t, N_HEADS_A, HEAD_DIM_A),
            c_new, n_new[:, :, 0, :], m_new[:, :, 0, 0], conv_new, extra)


def kernel(x_prompt, x_sample, cache_k, cache_v, page_table, state_C, state_n, state_m, state_conv,
           c_prompt, c_sample, w_ada, b_ada, g_norm_mix, w_in, b_gates, w_out, g_norm_ffn,
           w_up, w_conv, b_conv, w_down, g_norm_final):
    depth = w_in.shape[0]
    assert depth == 1
    bp, tp, d = x_prompt.shape
    bs, ts, _ = x_sample.shape
    past = page_table.shape[1] * cache_k.shape[2]
    dl = 0
    mod = _ada(jnp.concatenate([c_prompt, c_sample], axis=0), w_ada[dl], b_ada[dl])
    w_main, w_gate = _pack_w_in(w_in[dl])
    b_gate = jnp.zeros((1, LANES), F32).at[0, :2 * N_HEADS_B].set(b_gates[dl])
    w = (g_norm_mix[dl].reshape(1, d), w_main, w_gate, b_gate, w_out[dl].astype(BF16),
         g_norm_ffn[dl].reshape(1, d), w_up[dl].astype(BF16), w_conv[dl], b_conv[dl].reshape(1, -1),
         w_down[dl].astype(BF16), g_norm_final.reshape(1, d))
    yp, kp, vp, cp, np_, mp, cvp, cache_kmean = _layer(
        x_prompt, mod[:bp], jnp.arange(tp, dtype=jnp.int32), w, cache=(cache_k[dl], page_table))
    ys, ks, vs, cs, ns, ms, cvs, _ = _layer(
        x_sample, mod[bp:], past + jnp.arange(ts, dtype=jnp.int32), w,
        sample=(cache_k[dl], cache_v[dl], page_table, cache_kmean,
                state_C[dl], state_n[dl], state_m[dl], state_conv[dl]))
    st = lambda a: a[None]
    return (yp, ys, st(kp), st(vp), st(ks), st(vs), st(cp), st(np_), st(mp), st(cs), st(ns), st(ms),
            st(cvp), st(cvs))
```

```python
import functools
import math

import jax
import jax.numpy as jnp
from jax import lax
from jax.experimental import pallas as pl
from jax.experimental.pallas import tpu as pltpu

F32 = jnp.float32
BF16 = jnp.bfloat16

N_HEADS_A = 8
HEAD_DIM_A = 128
MOBA_BLOCK = 256
MOBA_TOPK = 3
ROPE_THETA = 10000.0
N_HEADS_B = 4
DK_B = 128
DV_B = 256
CONV_W = 3
RMS_EPS = 1e-6
MASKED = -1e30
LANES = 128
SUBLANES = 8
ROW_TILE = 256
MLSTM_CHUNK = 256
SAMPLE_CHUNK = 128
FFN_CHUNK = 768
VMEM_LIMIT = 56 * 1024 * 1024


def _resident(shape):
    nd = len(shape)
    return pl.BlockSpec(shape, lambda *_: (0,) * nd, pipeline_mode=pl.Buffered(1))


def _params(sem):
    return pltpu.CompilerParams(dimension_semantics=sem, vmem_limit_bytes=VMEM_LIMIT)


def _split_bf16(x, pieces):
    out = []
    r = x
    for _ in range(pieces):
        p = r.astype(BF16)
        out.append(p)
        r = r - p.astype(F32)
    return out


def _dot_nt(a, b):
    return lax.dot_general(a, b, (((1,), (1,)), ((), ())), preferred_element_type=F32)


def _dot_tn(a, b):
    return lax.dot_general(a, b, (((0,), (0,)), ((), ())), preferred_element_type=F32)


def _dot(a, b):
    return jnp.dot(a, b, preferred_element_type=F32)


def _ada_kernel(c_ref, w_ref, b_ref, o_ref):
    c = c_ref[...]
    s = (c * jax.nn.sigmoid(c)).astype(BF16)
    o_ref[...] = _dot(s, w_ref[...].astype(BF16)) + b_ref[...]


def _ada(c, w_ada, b_ada):
    n, d = c.shape
    d_out = w_ada.shape[1]
    tn = d
    return pl.pallas_call(
        _ada_kernel,
        out_shape=jax.ShapeDtypeStruct((n, d_out), F32),
        grid=(d_out // tn,),
        in_specs=[pl.BlockSpec((n, d), lambda j: (0, 0)),
                  pl.BlockSpec((d, tn), lambda j: (0, j)),
                  pl.BlockSpec((1, tn), lambda j: (0, j))],
        out_specs=pl.BlockSpec((n, tn), lambda j: (0, j)),
        compiler_params=_params(("arbitrary",)),
        name="ada_mod",
    )(c, w_ada, b_ada.reshape(1, d_out))


def _inproj_kernel(x_ref, g_ref, sh_ref, sc_ref, cos_ref, sin_ref, wa_ref, wb_ref, wg_ref, bg_ref,
                   q_ref, kf_ref, kb_ref, vf_ref, vb_ref, qkb_ref, vbb_ref, sga_ref, sgob_ref,
                   gc_ref, gr_ref, km_ref):
    tm, d = x_ref.shape
    wa = N_HEADS_A * HEAD_DIM_A
    x = x_ref[...]
    ms = jnp.mean(x * x, axis=-1, keepdims=True)
    xn = x * lax.rsqrt(ms + RMS_EPS) * g_ref[...]
    h = (xn * (1.0 + sc_ref[...]) + sh_ref[...]).astype(BF16)
    cos = cos_ref[...]
    sin = sin_ref[...]

    n_first = wa_ref.shape[1] // wa

    def proj(c):
        if c < n_first:
            return _dot(h, wa_ref[:, c * wa:(c + 1) * wa])
        return _dot(h, wb_ref[:, (c - n_first) * wa:(c - n_first + 1) * wa])

    def rope(a, hh):
        xh = a[:, hh * HEAD_DIM_A:(hh + 1) * HEAD_DIM_A]
        return xh * cos + pltpu.roll(xh, HEAD_DIM_A // 2, axis=1) * sin

    qscale = HEAD_DIM_A ** -0.5 * math.log2(math.e)
    a = proj(0)
    for hh in range(N_HEADS_A):
        sl = slice(hh * HEAD_DIM_A, (hh + 1) * HEAD_DIM_A)
        q_ref[:, sl] = (rope(a, hh) * qscale).astype(BF16)
    a = proj(1)
    for hh in range(N_HEADS_A):
        sl = slice(hh * HEAD_DIM_A, (hh + 1) * HEAD_DIM_A)
        kr = rope(a, hh)
        kf_ref[:, sl] = kr
        kb_ref[:, sl] = kr.astype(BF16)
        for g in range(tm // MOBA_BLOCK):
            km_ref[g, :, sl] = jnp.mean(kr[g * MOBA_BLOCK:(g + 1) * MOBA_BLOCK], axis=0, keepdims=True)
    a = proj(2)
    vf_ref[...] = a
    vb_ref[...] = a.astype(BF16)
    a = proj(3)
    wqk = N_HEADS_B * DK_B
    qkb_ref[:, :wqk] = a[:, :wqk].astype(BF16)
    qkb_ref[:, wqk:] = (a[:, wqk:] * (DK_B ** -0.5)).astype(BF16)
    vbb_ref[...] = proj(4).astype(BF16)
    so = jax.nn.sigmoid(proj(5))
    sga_ref[...] = jax.nn.sigmoid(proj(6)).astype(BF16)
    sgob_ref[...] = (jax.nn.sigmoid(proj(7)) * so).astype(BF16)
    g = _dot(h, wg_ref[...]) + bg_ref[...]
    lane = lax.broadcasted_iota(jnp.int32, g.shape, 1)
    logsig = jnp.minimum(g, 0.0) - jnp.log(1.0 + jnp.exp(-jnp.abs(g)))
    g = jnp.where(lane >= N_HEADS_B, logsig, g)
    gc_ref[...] = g
    gr_ref[...] = g.T[:2 * N_HEADS_B, :]


def _inproj(x, g_mix, shift, scale, cos2, sin2, w_main, w_gate, b_gate, *, rows_per_mod, rows_per_pos):
    n, d = x.shape
    tm = ROW_TILE
    assert n % tm == 0 and tm % MOBA_BLOCK == 0
    wa = N_HEADS_A * HEAD_DIM_A
    if rows_per_mod:
        assert rows_per_mod % tm == 0
        mod_spec = pl.BlockSpec((None, 1, d), lambda i: (i // (rows_per_mod // tm), 0, 0))
    else:
        mod_spec = pl.BlockSpec((tm, d), lambda i: (i, 0))
    assert rows_per_pos % tm == 0
    pos_spec = pl.BlockSpec((tm, HEAD_DIM_A), lambda i: (i % (rows_per_pos // tm), 0))
    row = lambda w: pl.BlockSpec((tm, w), lambda i: (i, 0))
    nblk = tm // MOBA_BLOCK
    out_shape = (
        jax.ShapeDtypeStruct((n, wa), BF16),
        jax.ShapeDtypeStruct((n, wa), F32),
        jax.ShapeDtypeStruct((n, wa), BF16),
        jax.ShapeDtypeStruct((n, wa), F32),
        jax.ShapeDtypeStruct((n, wa), BF16),
        jax.ShapeDtypeStruct((n, 2 * N_HEADS_B * DK_B), BF16),
        jax.ShapeDtypeStruct((n, N_HEADS_B * DV_B), BF16),
        jax.ShapeDtypeStruct((n, wa), BF16),
        jax.ShapeDtypeStruct((n, N_HEADS_B * DV_B), BF16),
        jax.ShapeDtypeStruct((n, LANES), F32),
        jax.ShapeDtypeStruct((2 * N_HEADS_B, n), F32),
        jax.ShapeDtypeStruct((n // MOBA_BLOCK, 1, wa), F32),
    )
    out_specs = (row(wa), row(wa), row(wa), row(wa), row(wa), row(2 * N_HEADS_B * DK_B), row(N_HEADS_B * DV_B),
                 row(wa), row(N_HEADS_B * DV_B), row(LANES),
                 pl.BlockSpec((2 * N_HEADS_B, tm), lambda i: (0, i)),
                 pl.BlockSpec((nblk, 1, wa), lambda i: (i, 0, 0)))
    return pl.pallas_call(
        _inproj_kernel,
        out_shape=out_shape,
        grid=(n // tm,),
        in_specs=[row(d), _resident((1, d)), mod_spec, mod_spec, pos_spec, pos_spec,
                  _resident(w_main[0].shape), _resident(w_main[1].shape), _resident(w_gate.shape),
                  _resident((1, LANES))],
        out_specs=out_specs,
        compiler_params=_params(("parallel",)),
        name="in_proj",
    )(x, g_mix, shift, scale, cos2, sin2, *w_main, w_gate, b_gate)


def _topk_mask(gate, valid, axis):
    blk = lax.broadcasted_iota(jnp.int32, gate.shape, axis).astype(F32)
    g = jnp.where(valid, gate, -jnp.inf)
    sel = jnp.zeros(gate.shape, F32)
    for _ in range(MOBA_TOPK):
        mx = jnp.max(g, axis=axis, keepdims=True)
        idx = jnp.min(jnp.where(g == mx, blk, float(gate.shape[axis])), axis=axis, keepdims=True)
        pick = blk == idx
        sel = jnp.where(pick, jnp.where(valid, 1.0, 0.0), sel)
        g = jnp.where(pick, -jnp.inf, g)
    return sel


def _gate_scores(q, kmean):
    hi, lo = _split_bf16(kmean, 2)
    return _dot_nt(q, hi) + _dot_nt(q, lo)


def _gate_scores_t(q, kmean):
    hi, lo = _split_bf16(kmean, 2)
    return _dot_nt(hi, q) + _dot_nt(lo, q)


def _moba_prompt_kernel(pt_ref, q_ref, k_ref, v_ref, km_ref, ck_ref, o_ref, cm_ref,
                        kaug, vext, qa, s0, s1, mx0, mx1, m_ref, acc_ref, pbuf, psem, *, pages_per_step, tk):
    i = pl.program_id(2)
    tq, hd = q_ref.shape
    t = k_ref.shape[0]
    n_own = tq // tk
    step = (pl.program_id(0) * pl.num_programs(1) + pl.program_id(1)) * pl.num_programs(2) + i
    n_steps = pl.num_programs(0) * pl.num_programs(1) * pl.num_programs(2)

    def page_copy(stp, p):
        half = stp % 2
        return pltpu.make_async_copy(ck_ref.at[pt_ref[stp * pages_per_step + p]],
                                     pbuf.at[half * pages_per_step + p], psem.at[half])

    @pl.when(i == 0)
    def _():
        vext[:, :hd] = v_ref[...]
        vext[:, hd:] = jnp.ones((t, hd), BF16)
        rowi = lax.broadcasted_iota(jnp.int32, (hd, tq), 0)
        coli = lax.broadcasted_iota(jnp.int32, (hd, tq), 1)
        for j in range(t // tq):
            kaug[:hd, j * tq:(j + 1) * tq] = k_ref[j * tq:(j + 1) * tq, :].T
            blk_of_key = j * (tq // MOBA_BLOCK) + coli // MOBA_BLOCK
            kaug[hd:, j * tq:(j + 1) * tq] = jnp.where(rowi == blk_of_key, 1.0, 0.0).astype(BF16)

    @pl.when(step == 0)
    def _():
        for p in range(pages_per_step):
            page_copy(step, p).start()

    @pl.when(step + 1 < n_steps)
    def _():
        for p in range(pages_per_step):
            page_copy(step + 1, p).start()

    for p in range(pages_per_step):
        page_copy(step, p).wait()
    per_blk = MOBA_BLOCK // pbuf.shape[1]
    for r in range(pages_per_step // per_blk):
        base = (step % 2) * pages_per_step + r * per_blk
        tot = jnp.sum(pbuf[base], axis=0)
        for p in range(1, per_blk):
            tot = tot + jnp.sum(pbuf[base + p], axis=0)
        cm_ref[r] = tot * (1.0 / MOBA_BLOCK)

    q = q_ref[...]
    nbp = -(-(t // MOBA_BLOCK) // SUBLANES) * SUBLANES
    gate = _gate_scores_t(q, km_ref[:nbp, :])
    blk = lax.broadcasted_iota(jnp.int32, gate.shape, 0)
    own = i * (tq // MOBA_BLOCK) + lax.broadcasted_iota(jnp.int32, gate.shape, 1) // MOBA_BLOCK
    sel = _topk_mask(gate, blk < own, 0)
    bias = jnp.where(blk == own, 0.0, jnp.where(sel > 0.5, 0.0, MASKED))
    bias = jnp.concatenate([bias, jnp.zeros((LANES - nbp, tq), F32)], axis=0)
    qa[:, :hd] = q
    qa[:, hd:] = bias.T.astype(BF16)
    m_ref[...] = jnp.full(m_ref.shape, MASKED, F32)
    acc_ref[...] = jnp.zeros(acc_ref.shape, F32)

    n_lt = tk // LANES

    def lane_tile_max(s):
        pm = s[:, :LANES]
        for j in range(1, n_lt):
            pm = jnp.maximum(pm, s[:, j * LANES:(j + 1) * LANES])
        return pm

    def scores_to(slot, e, row0=0):
        s_ref, mx_ref = slot
        st = pl.multiple_of(e * tk, tk)
        s = _dot(qa[row0:, :], kaug[:, pl.ds(st, tk)])
        s_ref[row0:, :] = s
        mx_ref[row0:, :] = lane_tile_max(s)

    def absorb(slot, e, row0=0):
        s_ref, mx_ref = slot
        st = pl.multiple_of(e * tk, tk)
        m = m_ref[row0:, :]
        m_new = jnp.maximum(m, jnp.broadcast_to(jnp.max(mx_ref[row0:, :], axis=1, keepdims=True), m.shape))
        p = jnp.concatenate([jnp.exp2(s_ref[row0:, j * LANES:(j + 1) * LANES] - m_new).astype(BF16)
                             for j in range(n_lt)], axis=1)
        alpha = jnp.exp2(m - m_new)
        pv = _dot(p, vext[pl.ds(st, tk), :])
        acc_ref[row0:, :hd] = alpha * acc_ref[row0:, :hd] + pv[:, :hd]
        acc_ref[row0:, hd:] = alpha * acc_ref[row0:, hd:] + pv[:, hd:]
        m_ref[row0:, :] = m_new

    def absorb_own(slot, j):
        s_ref, mx_ref = slot
        rows = slice(j * tk, (j + 1) * tk)
        r = lax.broadcasted_iota(jnp.int32, (tk, tk), 0)
        c = lax.broadcasted_iota(jnp.int32, (tk, tk), 1)
        s = jnp.where(c <= r, s_ref[rows, :], MASKED)
        s_ref[rows, :] = s
        mx_ref[rows, :] = lane_tile_max(s)
        absorb(slot, i * n_own + j, row0=j * tk)

    slot0, slot1 = (s0, mx0), (s1, mx1)
    scores_to(slot0, 0)

    def pair(p, carry):
        e = 2 * p
        scores_to(slot1, e + 1)
        absorb(slot0, e)
        scores_to(slot0, e + 2)
        absorb(slot1, e + 1)
        return carry

    lax.fori_loop(0, i * (n_own // 2), pair, 0)

    slots = (slot0, slot1)
    for j in range(n_own):
        if j + 1 < n_own:
            scores_to(slots[(j + 1) % 2], i * n_own + j + 1, row0=(j + 1) * tk)
        absorb_own(slots[j % 2], j)
    o_ref[...] = (acc_ref[:, :hd] / acc_ref[:, hd:]).astype(o_ref.dtype)


def _moba_prompt(q, k, v, kmean, cache_k, page_table, *, blocks_per_key_tile=2, key_tiles_per_query_tile=2):
    b, t, w = q.shape
    tk = blocks_per_key_tile * MOBA_BLOCK
    tq = key_tiles_per_query_tile * tk
    hd = HEAD_DIM_A
    nh = w // hd
    assert key_tiles_per_query_tile % 2 == 0
    assert t % tq == 0 and t // MOBA_BLOCK <= LANES and kmean.shape[1] == LANES
    n_steps = b * nh * (t // tq)
    bs, n_pages = page_table.shape
    page = cache_k.shape[1]
    per_blk = MOBA_BLOCK // page
    pages_per_step = bs * n_pages // n_steps
    assert pages_per_step * n_steps == bs * n_pages and pages_per_step % per_blk == 0 and MOBA_BLOCK % page == 0
    blk_per_step = pages_per_step // per_blk

    def step_of(bb, hh, i):
        return (bb * nh + hh) * (t // tq) + i

    ya, cmean = pl.pallas_call(
        functools.partial(_moba_prompt_kernel, pages_per_step=pages_per_step, tk=tk),
        out_shape=(jax.ShapeDtypeStruct((b, t, w), BF16),
                   jax.ShapeDtypeStruct((n_steps, blk_per_step) + cache_k.shape[2:], F32)),
        grid_spec=pltpu.PrefetchScalarGridSpec(
            num_scalar_prefetch=1,
            grid=(b, nh, t // tq),
            in_specs=[pl.BlockSpec((None, tq, hd), lambda bb, hh, i, pt: (bb, i, hh)),
                      pl.BlockSpec((None, t, hd), lambda bb, hh, i, pt: (bb, 0, hh)),
                      pl.BlockSpec((None, t, hd), lambda bb, hh, i, pt: (bb, 0, hh)),
                      pl.BlockSpec((None, LANES, hd), lambda bb, hh, i, pt: (bb, 0, hh)),
                      pl.BlockSpec(memory_space=pl.ANY)],
            out_specs=(pl.BlockSpec((None, tq, hd), lambda bb, hh, i, pt: (bb, i, hh)),
                       pl.BlockSpec((None, blk_per_step) + cache_k.shape[2:],
                                    lambda bb, hh, i, pt: (step_of(bb, hh, i), 0, 0, 0))),
            scratch_shapes=[pltpu.VMEM((2 * hd, t), BF16), pltpu.VMEM((t, 2 * hd), BF16),
                            pltpu.VMEM((tq, 2 * hd), BF16),
                            pltpu.VMEM((tq, tk), F32), pltpu.VMEM((tq, tk), F32),
                            pltpu.VMEM((tq, LANES), F32), pltpu.VMEM((tq, LANES), F32),
                            pltpu.VMEM((tq, LANES), F32), pltpu.VMEM((tq, 2 * hd), F32),
                            pltpu.VMEM((2 * pages_per_step,) + cache_k.shape[1:], F32),
                            pltpu.SemaphoreType.DMA((2,))]),
        compiler_params=_params(("arbitrary", "arbitrary", "arbitrary")),
        name="moba_prompt",
    )(page_table.reshape(-1), q, k, v, kmean, cache_k)
    return ya, cmean.reshape((bs, n_pages // per_blk) + cache_k.shape[2:])


def _mlstm_kernel(qk_ref, v_ref, gc_ref, gr_ref, c0_ref, n0_ref, m0_ref,
                  h_ref, c_out, n_out, m_out, c_s, n_s, m_s, *, valid_rows):
    ci = pl.program_id(1)
    L = qk_ref.shape[0]
    wqk = N_HEADS_B * DK_B

    @pl.when(ci == 0)
    def _():
        c_s[...] = c0_ref[...]
        n_s[...] = n0_ref[...]
        m_s[...] = m0_ref[...]

    gc = gc_ref[...]
    gr = gr_ref[...]
    if valid_rows < L:
        rr = lax.broadcasted_iota(jnp.int32, gc.shape, 0)
        ln = lax.broadcasted_iota(jnp.int32, gc.shape, 1)
        gc = jnp.where(rr < valid_rows, gc, jnp.where(ln < N_HEADS_B, MASKED, 0.0))
        cc = lax.broadcasted_iota(jnp.int32, gr.shape, 1)
        rw = lax.broadcasted_iota(jnp.int32, gr.shape, 0)
        gr = jnp.where(cc < valid_rows, gr, jnp.where(rw < N_HEADS_B, MASKED, 0.0))

    r = lax.broadcasted_iota(jnp.int32, (L, L), 0)
    c = lax.broadcasted_iota(jnp.int32, (L, L), 1)
    causal = c <= r
    tri = jnp.where(causal, 1.0, 0.0).astype(BF16)
    triu = jnp.where(r <= c, 1.0, 0.0).astype(BF16)
    bc_all = sum(_dot(tri, p) for p in _split_bf16(gc, 3))
    br_all = sum(_dot(p, triu) for p in _split_bf16(gr, 3))

    def rep(x, width):
        return jnp.concatenate([x] * (width // LANES), axis=1)

    ones = jnp.ones((L, LANES), BF16)
    for hh in range(N_HEADS_B):
        q = qk_ref[:, hh * DK_B:(hh + 1) * DK_B]
        k = qk_ref[:, wqk + hh * DK_B:wqk + (hh + 1) * DK_B]
        v1 = jnp.concatenate([v_ref[:, hh * DV_B:(hh + 1) * DV_B], ones], axis=1)
        i_col = jnp.broadcast_to(gc[:, hh:hh + 1], (L, LANES))
        b_col = jnp.broadcast_to(bc_all[:, N_HEADS_B + hh:N_HEADS_B + hh + 1], (L, LANES))
        i_row = gr[hh:hh + 1, :]
        b_row = br_all[N_HEADS_B + hh:N_HEADS_B + hh + 1, :]
        C = c_s[hh]
        n = n_s[hh]
        m_prev = m_s[hh]

        dmat = jnp.where(causal, rep(b_col, L) - b_row + i_row, MASKED)
        inter = b_col + m_prev
        m_t = jnp.maximum(inter, jnp.broadcast_to(jnp.max(dmat, axis=1, keepdims=True), (L, LANES)))
        w = jnp.exp(dmat - rep(m_t, L))
        a_inter = jnp.exp(inter - m_t)
        qk = (_dot_nt(q, k) * w).astype(BF16)
        cn = jnp.concatenate([C, jnp.broadcast_to(n, (LANES, DK_B))], axis=0).astype(BF16)
        qc = _dot_nt(q, cn)
        qv = _dot(qk, v1)
        num = rep(a_inter, DV_B) * qc[:, :DV_B] + qv[:, :DV_B]
        den = a_inter * qc[:, DV_B:] + qv[:, DV_B:]
        inv = 1.0 / jnp.maximum(jnp.abs(den), jnp.exp(-m_t))
        h_ref[:, hh * DV_B:(hh + 1) * DV_B] = (num * rep(inv, DV_B)).astype(h_ref.dtype)

        b_last = b_col[L - 1:L, :]
        g_col = b_last - b_col + i_col
        m_new = jnp.maximum(b_last + m_prev, jnp.max(g_col, axis=0, keepdims=True))
        decay = jnp.exp(b_last + m_prev - m_new)
        kw = k.astype(F32) * jnp.exp(g_col - m_new)
        c_s[hh] = decay * C + _dot_tn(v1[:, :DV_B], kw.astype(BF16))
        n_s[hh] = decay * n + jnp.sum(kw, axis=0, keepdims=True)
        m_s[hh] = m_new

    @pl.when(ci == pl.num_programs(1) - 1)
    def _():
        c_out[...] = c_s[...]
        n_out[...] = n_s[...]
        m_out[...] = m_s[...]


def _mlstm(qkb, vb, gates_col, gates_row, c0, n0, m0, *, chunk, valid_rows):
    b, t, _ = qkb.shape
    nc = t // chunk
    hb = N_HEADS_B
    st4 = lambda shp: pl.BlockSpec((None,) + shp, lambda bb, ci: (bb, 0, 0, 0))
    kern = functools.partial(_mlstm_kernel, valid_rows=valid_rows)
    return pl.pallas_call(
        kern,
        out_shape=(jax.ShapeDtypeStruct((b, t, hb * DV_B), BF16),
                   jax.ShapeDtypeStruct(c0.shape, F32),
                   jax.ShapeDtypeStruct(n0.shape, F32),
                   jax.ShapeDtypeStruct(m0.shape, F32)),
        grid=(b, nc),
        in_specs=[pl.BlockSpec((None, chunk, qkb.shape[2]), lambda bb, ci: (bb, ci, 0)),
                  pl.BlockSpec((None, chunk, vb.shape[2]), lambda bb, ci: (bb, ci, 0)),
                  pl.BlockSpec((None, chunk, LANES), lambda bb, ci: (bb, ci, 0)),
                  pl.BlockSpec((None, 2 * hb, chunk), lambda bb, ci: (bb * nc + ci, 0, 0)),
                  st4((hb, DV_B, DK_B)), st4((hb, 1, DK_B)), st4((hb, 1, LANES))],
        out_specs=(pl.BlockSpec((None, chunk, hb * DV_B), lambda bb, ci: (bb, ci, 0)),
                   st4((hb, DV_B, DK_B)), st4((hb, 1, DK_B)), st4((hb, 1, LANES))),
        scratch_shapes=[pltpu.VMEM((hb, DV_B, DK_B), F32),
                        pltpu.VMEM((hb, 1, DK_B), F32),
                        pltpu.VMEM((hb, 1, LANES), F32)],
        compiler_params=_params(("parallel", "arbitrary")),
        name="mlstm",
    )(qkb, vb, gates_col, gates_row, c0, n0, m0)


def _ffn_kernel(x_ref, ya_ref, hb_ref, sga_ref, sgob_ref, gtm_ref, shf_ref, scf_ref, gtf_ref,
                gffn_ref, gfin_ref, wout_ref, wup_ref, wconv_ref, bconv_ref, wdown_ref, *rest, seq_rows):
    if seq_rows:
        fix1_ref, fix2_ref, y_ref, a_ref = rest
    else:
        y_ref, tail_ref, carry = rest

        @pl.when(pl.program_id(1) == 0)
        def _():
            carry[...] = jnp.zeros_like(carry)
    tm = x_ref.shape[0]
    dff = wdown_ref.shape[0]
    merged = (sga_ref[...].astype(F32) * ya_ref[...].astype(F32)
              + sgob_ref[...].astype(F32) * hb_ref[...].astype(F32))
    x1 = x_ref[...] + gtm_ref[...] * _dot(merged.astype(BF16), wout_ref[...])
    ms = jnp.mean(x1 * x1, axis=-1, keepdims=True)
    xn = x1 * lax.rsqrt(ms + RMS_EPS) * gffn_ref[...]
    h = (xn * (1.0 + scf_ref[...]) + shf_ref[...]).astype(BF16)

    edges = list(range(0, dff, FFN_CHUNK)) + [dff]
    chunks = list(zip(edges[:-1], edges[1:]))

    def up(c0, c1):
        return _dot(h, wup_ref[:, c0:c1]), _dot(h, wup_ref[:, dff + c0:dff + c1])

    nxt = up(*chunks[0])
    f = None
    for ci, (c0, c1) in enumerate(chunks):
        a, gb = nxt
        if ci + 1 < len(chunks):
            nxt = up(*chunks[ci + 1])
        row = lax.broadcasted_iota(jnp.int32, a.shape, 0)
        r1 = pltpu.roll(a, 1, axis=0)
        r2 = pltpu.roll(a, 2, axis=0)
        if seq_rows:
            assert seq_rows & (seq_rows - 1) == 0
            t = row & (seq_rows - 1)
            a1 = jnp.where(t >= 1, r1, fix1_ref[:, c0:c1])
            a2 = jnp.where(t >= 2, r2, fix2_ref[:, c0:c1])
            a_ref[:, c0:c1] = a
        else:
            prev1 = carry[SUBLANES - 1:SUBLANES, c0:c1]
            prev2 = carry[SUBLANES - 2:SUBLANES - 1, c0:c1]
            a1 = jnp.where(row == 0, prev1, r1)
            a2 = jnp.where(row == 0, prev2, jnp.where(row == 1, prev1, r2))
            tail = a[tm - SUBLANES:, :]
            carry[:, c0:c1] = tail
            tail_ref[:, c0:c1] = tail
        ac = (bconv_ref[:, c0:c1] + a2 * wconv_ref[0:1, c0:c1] + a1 * wconv_ref[1:2, c0:c1]
              + a * wconv_ref[2:3, c0:c1])
        gelu = 0.5 * ac * (1.0 + jnp.tanh(math.sqrt(2.0 / math.pi) * (ac + 0.044715 * (ac * ac * ac))))
        part = _dot((gelu * gb).astype(BF16), wdown_ref[c0:c1, :])
        f = part if f is None else f + part
    x2 = x1 + gtf_ref[...] * f
    ms2 = jnp.mean(x2 * x2, axis=-1, keepdims=True)
    y_ref[...] = x2 * lax.rsqrt(ms2 + RMS_EPS) * gfin_ref[...]


def _ffn(x, ya, hb, sga, sgob, gtm, shf, scf, gtf, g_ffn, g_fin, w_out, w_up, w_conv, b_conv, w_down,
         *, batch, seq_rows=0, fix1=None, fix2=None):
    n, d = x.shape
    dff = w_down.shape[0]
    tm = ROW_TILE
    if seq_rows:
        assert n == tm
        grid = (1, 1)
        mod_spec = pl.BlockSpec((tm, d), lambda bb, i: (0, 0))
        nt = 1
    else:
        nt = n // batch // tm
        grid = (batch, nt)
        mod_spec = pl.BlockSpec((None, 1, d), lambda bb, i: (bb, 0, 0))
    row = lambda w: pl.BlockSpec((tm, w), lambda bb, i: (bb * nt + i, 0))
    in_specs = [row(d), row(d), row(d), row(d), row(d), mod_spec, mod_spec, mod_spec, mod_spec,
                _resident((1, d)), _resident((1, d)), _resident(w_out.shape), _resident(w_up.shape),
                _resident(w_conv.shape), _resident((1, dff)), _resident(w_down.shape)]
    args = [x, ya, hb, sga, sgob, gtm, shf, scf, gtf, g_ffn, g_fin, w_out, w_up, w_conv, b_conv, w_down]
    if seq_rows:
        in_specs += [row(dff), row(dff)]
        args += [fix1, fix2]
        out_shape = (jax.ShapeDtypeStruct((n, d), F32), jax.ShapeDtypeStruct((n, dff), F32))
        out_specs = (row(d), row(dff))
        scratch = []
    else:
        out_shape = (jax.ShapeDtypeStruct((n, d), F32), jax.ShapeDtypeStruct((batch, SUBLANES, dff), F32))
        out_specs = (row(d), pl.BlockSpec((None, SUBLANES, dff), lambda bb, i: (bb, 0, 0)))
        scratch = [pltpu.VMEM((SUBLANES, dff), F32)]
    return pl.pallas_call(
        functools.partial(_ffn_kernel, seq_rows=seq_rows),
        out_shape=out_shape,
        grid=grid,
        in_specs=in_specs,
        out_specs=out_specs,
        scratch_shapes=scratch,
        compiler_params=_params(("parallel", "arbitrary")),
        name="merge_ffn",
    )(*args)


def _select_kernel(q_ref, km_ref, o_ref):
    nseq, ts, _ = q_ref.shape
    n_blocks = km_ref.shape[1]
    lane = lax.broadcasted_iota(jnp.int32, (ts, LANES), 1)
    for sq in range(nseq):
        for hh in range(N_HEADS_A):
            sl = slice(hh * HEAD_DIM_A, (hh + 1) * HEAD_DIM_A)
            g = _gate_scores(q_ref[sq, :, sl].astype(BF16), km_ref[sq, :, hh, :])
            blk = lax.broadcasted_iota(jnp.int32, g.shape, 1).astype(F32)
            out = jnp.zeros((ts, LANES), F32)
            for r in range(MOBA_TOPK):
                mx = jnp.max(g, axis=1, keepdims=True)
                idx = jnp.min(jnp.where(g == mx, blk, float(n_blocks)), axis=1, keepdims=True)
                out = jnp.where(lane == r, idx, out)
                g = jnp.where(blk == idx, -jnp.inf, g)
            o_ref[sq, hh] = out.astype(jnp.int32)


def _select_blocks(q, kmean):
    b, ts, w = q.shape
    _, nb, nh, hd = kmean.shape
    nseq = math.gcd(b, 4)
    return pl.pallas_call(
        _select_kernel,
        out_shape=jax.ShapeDtypeStruct((b, N_HEADS_A, ts, LANES), jnp.int32),
        grid=(b // nseq,),
        in_specs=[pl.BlockSpec((nseq, ts, w), lambda bb: (bb, 0, 0)),
                  pl.BlockSpec((nseq, nb, nh, hd), lambda bb: (bb, 0, 0, 0))],
        out_specs=pl.BlockSpec((nseq, N_HEADS_A, ts, LANES), lambda bb: (bb, 0, 0, 0)),
        compiler_params=_params(("parallel",)),
        name="select_blocks",
    )(q, kmean)


def _moba_sample_kernel(sel_ref, pt_ref, q_ref, kn_ref, vn_ref, ck_ref, cv_ref, o_ref, kbuf, vbuf, sem,
                        *, n_pages, per_blk):
    b = pl.program_id(0)
    hh = pl.program_id(1)
    ts = q_ref.shape[0]
    page = kbuf.shape[1]
    nsel = MOBA_TOPK * per_blk

    step = b * N_HEADS_A + hh
    nsteps = pl.num_programs(0) * N_HEADS_A
    nslots = ts * nsel

    def start_all(stp):
        head = stp % N_HEADS_A
        half = stp % 2
        for j in range(nslots):
            blk = sel_ref[stp * (ts * MOBA_TOPK) + j // per_blk]
            pg = pt_ref[(stp // N_HEADS_A) * n_pages + blk * per_blk + j % per_blk]
            pltpu.make_async_copy(ck_ref.at[pg, :, head, :], kbuf.at[half * nslots + j], sem.at[0, half]).start()
            pltpu.make_async_copy(cv_ref.at[pg, :, head, :], vbuf.at[half * nslots + j], sem.at[1, half]).start()

    @pl.when(step == 0)
    def _():
        start_all(step)

    @pl.when(step + 1 < nsteps)
    def _():
        start_all(step + 1)

    base = (step % 2) * nslots
    for buf, kind in ((kbuf, 0), (vbuf, 1)):
        whole = buf.at[pl.ds(base, nslots)]
        pltpu.make_async_copy(whole, whole, sem.at[kind, step % 2]).wait()

    qf = q_ref[...]
    q = qf.astype(BF16)
    kn = kn_ref[...]
    vn = vn_ref[...]
    jrow = lax.broadcasted_iota(jnp.int32, (ts, LANES), 0)
    for t in range(ts):
        qrep = jnp.broadcast_to(q[t:t + 1, :], (LANES, q.shape[1]))
        kc = kbuf[pl.ds(base + t * nsel, nsel)].reshape(nsel * page, -1).astype(BF16)
        vc = vbuf[pl.ds(base + t * nsel, nsel)].reshape(nsel * page, -1)
        s = _dot_nt(kc, qrep)
        s_own = jnp.where(jrow <= t, jnp.sum(kn * qf[t:t + 1, :], axis=1, keepdims=True), MASKED)
        m = jnp.maximum(jnp.max(s, axis=0, keepdims=True), jnp.max(s_own, axis=0, keepdims=True))
        p = jnp.exp2(s - m)
        p_own = jnp.exp2(s_own - m)
        l = jnp.sum(p, axis=0, keepdims=True) + jnp.sum(p_own, axis=0, keepdims=True)
        o = jnp.sum(p * vc, axis=0, keepdims=True) + jnp.sum(p_own * vn, axis=0, keepdims=True)
        o_ref[t:t + 1, :] = o / l


def _moba_sample(q, k_new, v_new, cache_k, cache_v, page_table, sel):
    b, n_pages = page_table.shape
    ts = q.shape[0] // b
    _, page, nh, hd = cache_k.shape
    per_blk = MOBA_BLOCK // page
    nslots = ts * MOBA_TOPK * per_blk
    blk = pl.BlockSpec((ts, hd), lambda bb, hh, *_: (bb, hh))
    kern = functools.partial(_moba_sample_kernel, n_pages=n_pages, per_blk=per_blk)
    return pl.pallas_call(
        kern,
        out_shape=jax.ShapeDtypeStruct(q.shape, F32),
        grid_spec=pltpu.PrefetchScalarGridSpec(
            num_scalar_prefetch=2,
            grid=(b, nh),
            in_specs=[blk, blk, blk, pl.BlockSpec(memory_space=pl.ANY), pl.BlockSpec(memory_space=pl.ANY)],
            out_specs=blk,
            scratch_shapes=[pltpu.VMEM((2 * nslots, page, hd), F32), pltpu.VMEM((2 * nslots, page, hd), F32),
                            pltpu.SemaphoreType.DMA((2, 2))]),
        compiler_params=_params(("arbitrary", "arbitrary")),
        name="moba_sample",
    )(sel.reshape(-1), page_table.reshape(-1), q, k_new, v_new, cache_k, cache_v)


def _rope_tables(pos):
    half = HEAD_DIM_A // 2
    inv = ROPE_THETA ** (-jnp.arange(half, dtype=F32) / half)
    ang = pos.astype(F32)[:, None] * inv[None, :]
    cos, sin = jnp.cos(ang), jnp.sin(ang)
    return jnp.concatenate([cos, cos], axis=1), jnp.concatenate([-sin, sin], axis=1)


def _pack_w_in(w_in):
    d = w_in.shape[0]
    a = 3 * N_HEADS_A * HEAD_DIM_A + 2 * N_HEADS_B * DK_B + 2 * N_HEADS_B * DV_B
    ng = 2 * N_HEADS_B
    w_main = (w_in[:, :a].astype(BF16), w_in[:, a + ng:].astype(BF16))
    w_gate = jnp.zeros((d, LANES), BF16).at[:, :ng].set(w_in[:, a:a + ng].astype(BF16))
    return w_main, w_gate


def _layer(x, c_mod, pos, w, *, cache=None, sample=None):
    (g_mix, w_main, w_gate, b_gate, w_out, g_ffn, w_up, w_conv, b_conv, w_down, g_fin) = w
    b, t, d = x.shape
    n = b * t
    wa = N_HEADS_A * HEAD_DIM_A
    dff = w_down.shape[0]
    hb = N_HEADS_B
    sh_m, sc_m, gt_m, sh_f, sc_f, gt_f = jnp.split(c_mod, 6, axis=-1)
    cos2, sin2 = _rope_tables(pos)
    x2d = x.reshape(n, d)

    if sample is None:
        per_seq = lambda a: a.reshape(b, 1, d)
        q, kf, kb16, vf, vb16, qkb, vbb, sga, sgob, gcol, grow, kmean = _inproj(
            x2d, g_mix, per_seq(sh_m), per_seq(sc_m), cos2, sin2, w_main, w_gate, b_gate,
            rows_per_mod=t, rows_per_pos=t)
        kmean = jnp.pad(kmean.reshape(b, t // MOBA_BLOCK, wa), ((0, 0), (0, LANES - t // MOBA_BLOCK), (0, 0)))
        ya, extra = _moba_prompt(q.reshape(b, t, wa), kb16.reshape(b, t, wa), vb16.reshape(b, t, wa), kmean, *cache)
        ya = ya.reshape(n, wa)
        chunk = MLSTM_CHUNK
        nc = t // chunk
        grow3 = grow.reshape(2 * hb, b * nc, chunk).transpose(1, 0, 2)
        c0 = jnp.zeros((b, hb, DV_B, DK_B), F32)
        n0 = jnp.zeros((b, hb, 1, DK_B), F32)
        m0 = jnp.zeros((b, hb, 1, LANES), F32)
        hbo, c_new, n_new, m_new = _mlstm(qkb.reshape(b, t, -1), vbb.reshape(b, t, -1), gcol.reshape(b, t, LANES),
                                          grow3, c0, n0, m0, chunk=chunk, valid_rows=chunk)
        y, tail = _ffn(x2d, ya, hbo.reshape(n, -1), sga, sgob, per_seq(gt_m), per_seq(sh_f), per_seq(sc_f),
                       per_seq(gt_f), g_ffn, g_fin, w_out, w_up, w_conv, b_conv, w_down, batch=b)
        conv_new = tail[:, SUBLANES - (CONV_W - 1):, :]
    else:
        cache_k, cache_v, page_table, kmean, c_st, n_st, m_st, conv0 = sample
        extra = None
        assert n == ROW_TILE and t >= CONV_W - 1
        past = page_table.shape[1] * cache_k.shape[1]
        assert past % MOBA_BLOCK == 0 and t <= MOBA_BLOCK
        per_row = lambda a: jnp.repeat(a, t, axis=0)
        cos2, sin2 = jnp.tile(cos2, (b, 1)), jnp.tile(sin2, (b, 1))
        q, kf, kb16, vf, vb16, qkb, vbb, sga, sgob, gcol, grow, _ = _inproj(
            x2d, g_mix, per_row(sh_m), per_row(sc_m), cos2, sin2, w_main, w_gate, b_gate,
            rows_per_mod=0, rows_per_pos=n)
        qf = q.astype(F32)
        sel = _select_blocks(qf.reshape(b, t, wa), kmean)[..., :MOBA_TOPK]
        ya = _moba_sample(qf, kf, vf, cache_k, cache_v, page_table, sel).astype(BF16)
        chunk = SAMPLE_CHUNK
        pad = lambda a: jnp.pad(a.reshape(b, t, -1), ((0, 0), (0, chunk - t), (0, 0)))
        grow3 = jnp.pad(grow.reshape(2 * hb, b, t).transpose(1, 0, 2), ((0, 0), (0, 0), (0, chunk - t)))
        m0 = jnp.broadcast_to(m_st[:, :, None, None], (b, hb, 1, LANES))
        hbo, c_new, n_new, m_new = _mlstm(pad(qkb), pad(vbb), pad(gcol), grow3, c_st, n_st[:, :, None, :], m0,
                                          chunk=chunk, valid_rows=t)
        hbo = hbo[:, :t].reshape(n, -1)
        zero = jnp.zeros((b, t - 2, dff), F32)
        fix1 = jnp.concatenate([conv0[:, 1:2], zero, zero[:, :1]], axis=1).reshape(n, dff)
        fix2 = jnp.concatenate([conv0, zero], axis=1).reshape(n, dff)
        y, a_full = _ffn(x2d, ya, hbo, sga, sgob, per_row(gt_m), per_row(sh_f), per_row(sc_f), per_row(gt_f),
                         g_ffn, g_fin, w_out, w_up, w_conv, b_conv, w_down, batch=b, seq_rows=t,
                         fix1=fix1, fix2=fix2)
        conv_new = a_full.reshape(b, t, dff)[:, t - (CONV_W - 1):]
    return (y.reshape(b, t, d), kf.reshape(b, t, N_HEADS_A, HEAD_DIM_A), vf.reshape(b, t, N_HEADS_A, HEAD_DIM_A),
            c_new, n_new[:, :, 0, :], m_new[:, :, 0, 0], conv_new, extra)


def kernel(x_prompt, x_sample, cache_k, cache_v, page_table, state_C, state_n, state_m, state_conv,
           c_prompt, c_sample, w_ada, b_ada, g_norm_mix, w_in, b_gates, w_out, g_norm_ffn,
           w_up, w_conv, b_conv, w_down, g_norm_final):
    depth = w_in.shape[0]
    assert depth == 1
    bp, tp, d = x_prompt.shape
    bs, ts, _ = x_sample.shape
    past = page_table.shape[1] * cache_k.shape[2]
    dl = 0
    mod = _ada(jnp.concatenate([c_prompt, c_sample], axis=0), w_ada[dl], b_ada[dl])
    w_main, w_gate = _pack_w_in(w_in[dl])
    b_gate = jnp.zeros((1, LANES), F32).at[0, :2 * N_HEADS_B].set(b_gates[dl])
    w = (g_norm_mix[dl].reshape(1, d), w_main, w_gate, b_gate, w_out[dl].astype(BF16),
         g_norm_ffn[dl].reshape(1, d), w_up[dl].astype(BF16), w_conv[dl], b_conv[dl].reshape(1, -1),
         w_down[dl].astype(BF16), g_norm_final.reshape(1, d))
    yp, kp, vp, cp, np_, mp, cvp, cache_kmean = _layer(
        x_prompt, mod[:bp], jnp.arange(tp, dtype=jnp.int32), w, cache=(cache_k[dl], page_table))
    ys, ks, vs, cs, ns, ms, cvs, _ = _layer(
        x_sample, mod[bp:], past + jnp.arange(ts, dtype=jnp.int32), w,
        sample=(cache_k[dl], cache_v[dl], page_table, cache_kmean,
                state_C[dl], state_n[dl], state_m[dl], state_conv[dl]))
    st = lambda a: a[None]
    return (yp, ys, st(kp), st(vp), st(ks), st(vs), st(cp), st(np_), st(mp), st(cs), st(ns), st(ms),
            st(cvp), st(cvs))
```

```python
import functools
import math

import jax
import jax.numpy as jnp
from jax import lax
from jax.experimental import pallas as pl
from jax.experimental.pallas import tpu as pltpu

F32 = jnp.float32
BF16 = jnp.bfloat16

N_HEADS_A = 8
HEAD_DIM_A = 128
MOBA_BLOCK = 256
MOBA_TOPK = 3
ROPE_THETA = 10000.0
N_HEADS_B = 4
DK_B = 128
DV_B = 256
CONV_W = 3
RMS_EPS = 1e-6
MASKED = -1e30
LANES = 128
SUBLANES = 8
ROW_TILE = 256
MLSTM_CHUNK = 256
SAMPLE_CHUNK = 128
FFN_CHUNK = 768
VMEM_LIMIT = 56 * 1024 * 1024


def _resident(shape):
    nd = len(shape)
    return pl.BlockSpec(shape, lambda *_: (0,) * nd, pipeline_mode=pl.Buffered(1))


def _params(sem):
    return pltpu.CompilerParams(dimension_semantics=sem, vmem_limit_bytes=VMEM_LIMIT)


def _split_bf16(x, pieces):
    out = []
    r = x
    for _ in range(pieces):
        p = r.astype(BF16)
        out.append(p)
        r = r - p.astype(F32)
    return out


def _dot_nt(a, b):
    return lax.dot_general(a, b, (((1,), (1,)), ((), ())), preferred_element_type=F32)


def _dot_tn(a, b):
    return lax.dot_general(a, b, (((0,), (0,)), ((), ())), preferred_element_type=F32)


def _dot(a, b):
    return jnp.dot(a, b, preferred_element_type=F32)


def _ada_kernel(c_ref, w_ref, b_ref, o_ref):
    c = c_ref[...]
    s = (c * jax.nn.sigmoid(c)).astype(BF16)
    o_ref[...] = _dot(s, w_ref[...].astype(BF16)) + b_ref[...]


def _ada(c, w_ada, b_ada):
    n, d = c.shape
    d_out = w_ada.shape[1]
    tn = d
    return pl.pallas_call(
        _ada_kernel,
        out_shape=jax.ShapeDtypeStruct((n, d_out), F32),
        grid=(d_out // tn,),
        in_specs=[pl.BlockSpec((n, d), lambda j: (0, 0)),
                  pl.BlockSpec((d, tn), lambda j: (0, j)),
                  pl.BlockSpec((1, tn), lambda j: (0, j))],
        out_specs=pl.BlockSpec((n, tn), lambda j: (0, j)),
        compiler_params=_params(("arbitrary",)),
        name="ada_mod",
    )(c, w_ada, b_ada.reshape(1, d_out))


def _inproj_kernel(x_ref, g_ref, sh_ref, sc_ref, cos_ref, sin_ref, wa_ref, wb_ref, wg_ref, bg_ref,
                   q_ref, kf_ref, kb_ref, vf_ref, vb_ref, qkb_ref, vbb_ref, sga_ref, sgob_ref,
                   gc_ref, gr_ref, km_ref):
    tm, d = x_ref.shape
    wa = N_HEADS_A * HEAD_DIM_A
    x = x_ref[...]
    ms = jnp.mean(x * x, axis=-1, keepdims=True)
    xn = x * lax.rsqrt(ms + RMS_EPS) * g_ref[...]
    h = (xn * (1.0 + sc_ref[...]) + sh_ref[...]).astype(BF16)
    cos = cos_ref[...]
    sin = sin_ref[...]

    n_first = wa_ref.shape[1] // wa

    def proj(c):
        if c < n_first:
            return _dot(h, wa_ref[:, c * wa:(c + 1) * wa])
        return _dot(h, wb_ref[:, (c - n_first) * wa:(c - n_first + 1) * wa])

    def rope(a, hh):
        xh = a[:, hh * HEAD_DIM_A:(hh + 1) * HEAD_DIM_A]
        return xh * cos + pltpu.roll(xh, HEAD_DIM_A // 2, axis=1) * sin

    qscale = HEAD_DIM_A ** -0.5 * math.log2(math.e)
    a = proj(0)
    for hh in range(N_HEADS_A):
        sl = slice(hh * HEAD_DIM_A, (hh + 1) * HEAD_DIM_A)
        q_ref[:, sl] = (rope(a, hh) * qscale).astype(BF16)
    a = proj(1)
    for hh in range(N_HEADS_A):
        sl = slice(hh * HEAD_DIM_A, (hh + 1) * HEAD_DIM_A)
        kr = rope(a, hh)
        kf_ref[:, sl] = kr
        kb_ref[:, sl] = kr.astype(BF16)
        for g in range(tm // MOBA_BLOCK):
            km_ref[g, :, sl] = jnp.mean(kr[g * MOBA_BLOCK:(g + 1) * MOBA_BLOCK], axis=0, keepdims=True)
    a = proj(2)
    vf_ref[...] = a
    vb_ref[...] = a.astype(BF16)
    a = proj(3)
    wqk = N_HEADS_B * DK_B
    qkb_ref[:, :wqk] = a[:, :wqk].astype(BF16)
    qkb_ref[:, wqk:] = (a[:, wqk:] * (DK_B ** -0.5)).astype(BF16)
    vbb_ref[...] = proj(4).astype(BF16)
    so = jax.nn.sigmoid(proj(5))
    sga_ref[...] = jax.nn.sigmoid(proj(6)).astype(BF16)
    sgob_ref[...] = (jax.nn.sigmoid(proj(7)) * so).astype(BF16)
    g = _dot(h, wg_ref[...]) + bg_ref[...]
    lane = lax.broadcasted_iota(jnp.int32, g.shape, 1)
    logsig = jnp.minimum(g, 0.0) - jnp.log(1.0 + jnp.exp(-jnp.abs(g)))
    g = jnp.where(lane >= N_HEADS_B, logsig, g)
    gc_ref[...] = g
    gr_ref[...] = g.T[:2 * N_HEADS_B, :]


def _inproj(x, g_mix, shift, scale, cos2, sin2, w_main, w_gate, b_gate, *, rows_per_mod, rows_per_pos):
    n, d = x.shape
    tm = ROW_TILE
    assert n % tm == 0 and tm % MOBA_BLOCK == 0
    wa = N_HEADS_A * HEAD_DIM_A
    if rows_per_mod:
        assert rows_per_mod % tm == 0
        mod_spec = pl.BlockSpec((None, 1, d), lambda i: (i // (rows_per_mod // tm), 0, 0))
    else:
        mod_spec = pl.BlockSpec((tm, d), lambda i: (i, 0))
    assert rows_per_pos % tm == 0
    pos_spec = pl.BlockSpec((tm, HEAD_DIM_A), lambda i: (i % (rows_per_pos // tm), 0))
    row = lambda w: pl.BlockSpec((tm, w), lambda i: (i, 0))
    nblk = tm // MOBA_BLOCK
    out_shape = (
        jax.ShapeDtypeStruct((n, wa), BF16),
        jax.ShapeDtypeStruct((n, wa), F32),
        jax.ShapeDtypeStruct((n, wa), BF16),
        jax.ShapeDtypeStruct((n, wa), F32),
        jax.ShapeDtypeStruct((n, wa), BF16),
        jax.ShapeDtypeStruct((n, 2 * N_HEADS_B * DK_B), BF16),
        jax.ShapeDtypeStruct((n, N_HEADS_B * DV_B), BF16),
        jax.ShapeDtypeStruct((n, wa), BF16),
        jax.ShapeDtypeStruct((n, N_HEADS_B * DV_B), BF16),
        jax.ShapeDtypeStruct((n, LANES), F32),
        jax.ShapeDtypeStruct((2 * N_HEADS_B, n), F32),
        jax.ShapeDtypeStruct((n // MOBA_BLOCK, 1, wa), F32),
    )
    out_specs = (row(wa), row(wa), row(wa), row(wa), row(wa), row(2 * N_HEADS_B * DK_B), row(N_HEADS_B * DV_B),
                 row(wa), row(N_HEADS_B * DV_B), row(LANES),
                 pl.BlockSpec((2 * N_HEADS_B, tm), lambda i: (0, i)),
                 pl.BlockSpec((nblk, 1, wa), lambda i: (i, 0, 0)))
    return pl.pallas_call(
        _inproj_kernel,
        out_shape=out_shape,
        grid=(n // tm,),
        in_specs=[row(d), _resident((1, d)), mod_spec, mod_spec, pos_spec, pos_spec,
                  _resident(w_main[0].shape), _resident(w_main[1].shape), _resident(w_gate.shape),
                  _resident((1, LANES))],
        out_specs=out_specs,
        compiler_params=_params(("parallel",)),
        name="in_proj",
    )(x, g_mix, shift, scale, cos2, sin2, *w_main, w_gate, b_gate)


def _topk_mask(gate, valid, axis):
    blk = lax.broadcasted_iota(jnp.int32, gate.shape, axis).astype(F32)
    g = jnp.where(valid, gate, -jnp.inf)
    sel = jnp.zeros(gate.shape, F32)
    for _ in range(MOBA_TOPK):
        mx = jnp.max(g, axis=axis, keepdims=True)
        idx = jnp.min(jnp.where(g == mx, blk, float(gate.shape[axis])), axis=axis, keepdims=True)
        pick = blk == idx
        sel = jnp.where(pick, jnp.where(valid, 1.0, 0.0), sel)
        g = jnp.where(pick, -jnp.inf, g)
    return sel


def _gate_scores(q, kmean):
    hi, lo = _split_bf16(kmean, 2)
    return _dot_nt(q, hi) + _dot_nt(q, lo)


def _gate_scores_t(q, kmean):
    hi, lo = _split_bf16(kmean, 2)
    return _dot_nt(hi, q) + _dot_nt(lo, q)


def _moba_prompt_kernel(pt_ref, q_ref, k_ref, v_ref, km_ref, ck_ref, o_ref, cm_ref,
                        kaug, vext, qa, s0, s1, mx0, mx1, m_ref, acc_ref, pbuf, psem, *, pages_per_step, tk):
    i = pl.program_id(2)
    tq, hd = q_ref.shape
    t = k_ref.shape[0]
    n_own = tq // tk
    step = (pl.program_id(0) * pl.num_programs(1) + pl.program_id(1)) * pl.num_programs(2) + i
    n_steps = pl.num_programs(0) * pl.num_programs(1) * pl.num_programs(2)

    def page_copy(stp, p):
        half = stp % 2
        return pltpu.make_async_copy(ck_ref.at[pt_ref[stp * pages_per_step + p]],
                                     pbuf.at[half * pages_per_step + p], psem.at[half])

    @pl.when(i == 0)
    def _():
        vext[:, :hd] = v_ref[...]
        vext[:, hd:] = jnp.ones((t, hd), BF16)
        rowi = lax.broadcasted_iota(jnp.int32, (hd, tq), 0)
        coli = lax.broadcasted_iota(jnp.int32, (hd, tq), 1)
        for j in range(t // tq):
            kaug[:hd, j * tq:(j + 1) * tq] = k_ref[j * tq:(j + 1) * tq, :].T
            blk_of_key = j * (tq // MOBA_BLOCK) + coli // MOBA_BLOCK
            kaug[hd:, j * tq:(j + 1) * tq] = jnp.where(rowi == blk_of_key, 1.0, 0.0).astype(BF16)

    @pl.when(step == 0)
    def _():
        for p in range(pages_per_step):
            page_copy(step, p).start()

    @pl.when(step + 1 < n_steps)
    def _():
        for p in range(pages_per_step):
            page_copy(step + 1, p).start()

    for p in range(pages_per_step):
        page_copy(step, p).wait()
    per_blk = MOBA_BLOCK // pbuf.shape[1]
    for r in range(pages_per_step // per_blk):
        base = (step % 2) * pages_per_step + r * per_blk
        tot = jnp.sum(pbuf[base], axis=0)
        for p in range(1, per_blk):
            tot = tot + jnp.sum(pbuf[base + p], axis=0)
        cm_ref[r] = tot * (1.0 / MOBA_BLOCK)

    q = q_ref[...]
    nbp = -(-(t // MOBA_BLOCK) // SUBLANES) * SUBLANES
    gate = _gate_scores_t(q, km_ref[:nbp, :])
    blk = lax.broadcasted_iota(jnp.int32, gate.shape, 0)
    own = i * (tq // MOBA_BLOCK) + lax.broadcasted_iota(jnp.int32, gate.shape, 1) // MOBA_BLOCK
    sel = _topk_mask(gate, blk < own, 0)
    bias = jnp.where(blk == own, 0.0, jnp.where(sel > 0.5, 0.0, MASKED))
    bias = jnp.concatenate([bias, jnp.zeros((LANES - nbp, tq), F32)], axis=0)
    qa[:, :hd] = q
    qa[:, hd:] = bias.T.astype(BF16)
    m_ref[...] = jnp.full(m_ref.shape, MASKED, F32)
    acc_ref[...] = jnp.zeros(acc_ref.shape, F32)

    n_lt = tk // LANES

    def lane_tile_max(s):
        pm = s[:, :LANES]
        for j in range(1, n_lt):
            pm = jnp.maximum(pm, s[:, j * LANES:(j + 1) * LANES])
        return pm

    def scores_to(slot, e, row0=0):
        s_ref, mx_ref = slot
        st = pl.multiple_of(e * tk, tk)
        s = _dot(qa[row0:, :], kaug[:, pl.ds(st, tk)])
        s_ref[row0:, :] = s
        mx_ref[row0:, :] = lane_tile_max(s)

    def absorb(slot, e, row0=0):
        s_ref, mx_ref = slot
        st = pl.multiple_of(e * tk, tk)
        m = m_ref[row0:, :]
        m_new = jnp.maximum(m, jnp.broadcast_to(jnp.max(mx_ref[row0:, :], axis=1, keepdims=True), m.shape))
        p = jnp.concatenate([jnp.exp2(s_ref[row0:, j * LANES:(j + 1) * LANES] - m_new).astype(BF16)
                             for j in range(n_lt)], axis=1)
        alpha = jnp.exp2(m - m_new)
        pv = _dot(p, vext[pl.ds(st, tk), :])
        acc_ref[row0:, :hd] = alpha * acc_ref[row0:, :hd] + pv[:, :hd]
        acc_ref[row0:, hd:] = alpha * acc_ref[row0:, hd:] + pv[:, hd:]
        m_ref[row0:, :] = m_new

    def absorb_own(slot, j):
        s_ref, mx_ref = slot
        rows = slice(j * tk, (j + 1) * tk)
        r = lax.broadcasted_iota(jnp.int32, (tk, tk), 0)
        c = lax.broadcasted_iota(jnp.int32, (tk, tk), 1)
        s = jnp.where(c <= r, s_ref[rows, :], MASKED)
        s_ref[rows, :] = s
        mx_ref[rows, :] = lane_tile_max(s)
        absorb(slot, i * n_own + j, row0=j * tk)

    slot0, slot1 = (s0, mx0), (s1, mx1)
    scores_to(slot0, 0)

    def pair(p, carry):
        e = 2 * p
        scores_to(slot1, e + 1)
        absorb(slot0, e)
        scores_to(slot0, e + 2)
        absorb(slot1, e + 1)
        return carry

    lax.fori_loop(0, i * (n_own // 2), pair, 0)

    slots = (slot0, slot1)
    for j in range(n_own):
        if j + 1 < n_own:
            scores_to(slots[(j + 1) % 2], i * n_own + j + 1, row0=(j + 1) * tk)
        absorb_own(slots[j % 2], j)
    o_ref[...] = (acc_ref[:, :hd] / acc_ref[:, hd:]).astype(o_ref.dtype)


def _moba_prompt(q, k, v, kmean, cache_k, page_table, *, blocks_per_key_tile=2, key_tiles_per_query_tile=2):
    b, t, w = q.shape
    tk = blocks_per_key_tile * MOBA_BLOCK
    tq = key_tiles_per_query_tile * tk
    hd = HEAD_DIM_A
    nh = w // hd
    assert key_tiles_per_query_tile % 2 == 0
    assert t % tq == 0 and t // MOBA_BLOCK <= LANES and kmean.shape[1] == LANES
    n_steps = b * nh * (t // tq)
    bs, n_pages = page_table.shape
    page = cache_k.shape[1]
    per_blk = MOBA_BLOCK // page
    pages_per_step = bs * n_pages // n_steps
    assert pages_per_step * n_steps == bs * n_pages and pages_per_step % per_blk == 0 and MOBA_BLOCK % page == 0
    blk_per_step = pages_per_step // per_blk

    def step_of(bb, hh, i):
        return (bb * nh + hh) * (t // tq) + i

    ya, cmean = pl.pallas_call(
        functools.partial(_moba_prompt_kernel, pages_per_step=pages_per_step, tk=tk),
        out_shape=(jax.ShapeDtypeStruct((b, t, w), BF16),
                   jax.ShapeDtypeStruct((n_steps, blk_per_step) + cache_k.shape[2:], F32)),
        grid_spec=pltpu.PrefetchScalarGridSpec(
            num_scalar_prefetch=1,
            grid=(b, nh, t // tq),
            in_specs=[pl.BlockSpec((None, tq, hd), lambda bb, hh, i, pt: (bb, i, hh)),
                      pl.BlockSpec((None, t, hd), lambda bb, hh, i, pt: (bb, 0, hh)),
                      pl.BlockSpec((None, t, hd), lambda bb, hh, i, pt: (bb, 0, hh)),
                      pl.BlockSpec((None, LANES, hd), lambda bb, hh, i, pt: (bb, 0, hh)),
                      pl.BlockSpec(memory_space=pl.ANY)],
            out_specs=(pl.BlockSpec((None, tq, hd), lambda bb, hh, i, pt: (bb, i, hh)),
                       pl.BlockSpec((None, blk_per_step) + cache_k.shape[2:],
                                    lambda bb, hh, i, pt: (step_of(bb, hh, i), 0, 0, 0))),
            scratch_shapes=[pltpu.VMEM((2 * hd, t), BF16), pltpu.VMEM((t, 2 * hd), BF16),
                            pltpu.VMEM((tq, 2 * hd), BF16),
                            pltpu.VMEM((tq, tk), F32), pltpu.VMEM((tq, tk), F32),
                            pltpu.VMEM((tq, LANES), F32), pltpu.VMEM((tq, LANES), F32),
                            pltpu.VMEM((tq, LANES), F32), pltpu.VMEM((tq, 2 * hd), F32),
                            pltpu.VMEM((2 * pages_per_step,) + cache_k.shape[1:], F32),
                            pltpu.SemaphoreType.DMA((2,))]),
        compiler_params=_params(("arbitrary", "arbitrary", "arbitrary")),
        name="moba_prompt",
    )(page_table.reshape(-1), q, k, v, kmean, cache_k)
    return ya, cmean.reshape((bs, n_pages // per_blk) + cache_k.shape[2:])


def _mlstm_kernel(qk_ref, v_ref, gc_ref, gr_ref, c0_ref, n0_ref, m0_ref,
                  h_ref, c_out, n_out, m_out, c_s, n_s, m_s, *, valid_rows):
    ci = pl.program_id(1)
    L = qk_ref.shape[0]
    wqk = N_HEADS_B * DK_B

    @pl.when(ci == 0)
    def _():
        c_s[...] = c0_ref[...]
        n_s[...] = n0_ref[...]
        m_s[...] = m0_ref[...]

    gc = gc_ref[...]
    gr = gr_ref[...]
    if valid_rows < L:
        rr = lax.broadcasted_iota(jnp.int32, gc.shape, 0)
        ln = lax.broadcasted_iota(jnp.int32, gc.shape, 1)
        gc = jnp.where(rr < valid_rows, gc, jnp.where(ln < N_HEADS_B, MASKED, 0.0))
        cc = lax.broadcasted_iota(jnp.int32, gr.shape, 1)
        rw = lax.broadcasted_iota(jnp.int32, gr.shape, 0)
        gr = jnp.where(cc < valid_rows, gr, jnp.where(rw < N_HEADS_B, MASKED, 0.0))

    r = lax.broadcasted_iota(jnp.int32, (L, L), 0)
    c = lax.broadcasted_iota(jnp.int32, (L, L), 1)
    causal = c <= r
    tri = jnp.where(causal, 1.0, 0.0).astype(BF16)
    triu = jnp.where(r <= c, 1.0, 0.0).astype(BF16)
    bc_all = sum(_dot(tri, p) for p in _split_bf16(gc, 3))
    br_all = sum(_dot(p, triu) for p in _split_bf16(gr, 3))

    def rep(x, width):
        return jnp.concatenate([x] * (width // LANES), axis=1)

    ones = jnp.ones((L, LANES), BF16)
    pre = []
    for hh in range(N_HEADS_B):
        q = qk_ref[:, hh * DK_B:(hh + 1) * DK_B]
        k = qk_ref[:, wqk + hh * DK_B:wqk + (hh + 1) * DK_B]
        cn = jnp.concatenate([c_s[hh], jnp.broadcast_to(n_s[hh], (LANES, DK_B))], axis=0).astype(BF16)
        pre.append((_dot_nt(q, k), _dot_nt(q, cn)))
    for hh in range(N_HEADS_B):
        k = qk_ref[:, wqk + hh * DK_B:wqk + (hh + 1) * DK_B]
        s_qk, qc = pre[hh]
        v1 = jnp.concatenate([v_ref[:, hh * DV_B:(hh + 1) * DV_B], ones], axis=1)
        i_col = jnp.broadcast_to(gc[:, hh:hh + 1], (L, LANES))
        b_col = jnp.broadcast_to(bc_all[:, N_HEADS_B + hh:N_HEADS_B + hh + 1], (L, LANES))
        i_row = gr[hh:hh + 1, :]
        b_row = br_all[N_HEADS_B + hh:N_HEADS_B + hh + 1, :]
        C = c_s[hh]
        n = n_s[hh]
        m_prev = m_s[hh]

        dmat = jnp.where(causal, rep(b_col, L) - b_row + i_row, MASKED)
        inter = b_col + m_prev
        m_t = jnp.maximum(inter, jnp.broadcast_to(jnp.max(dmat, axis=1, keepdims=True), (L, LANES)))
        w = jnp.exp(dmat - rep(m_t, L))
        a_inter = jnp.exp(inter - m_t)
        qk = (s_qk * w).astype(BF16)
        qv = _dot(qk, v1)
        num = rep(a_inter, DV_B) * qc[:, :DV_B] + qv[:, :DV_B]
        den = a_inter * qc[:, DV_B:] + qv[:, DV_B:]
        inv = 1.0 / jnp.maximum(jnp.abs(den), jnp.exp(-m_t))
        h_ref[:, hh * DV_B:(hh + 1) * DV_B] = (num * rep(inv, DV_B)).astype(h_ref.dtype)

        b_last = b_col[L - 1:L, :]
        g_col = b_last - b_col + i_col
        m_new = jnp.maximum(b_last + m_prev, jnp.max(g_col, axis=0, keepdims=True))
        decay = jnp.exp(b_last + m_prev - m_new)
        kw = k.astype(F32) * jnp.exp(g_col - m_new)
        c_s[hh] = decay * C + _dot_tn(v1[:, :DV_B], kw.astype(BF16))
        n_s[hh] = decay * n + jnp.sum(kw, axis=0, keepdims=True)
        m_s[hh] = m_new

    @pl.when(ci == pl.num_programs(1) - 1)
    def _():
        c_out[...] = c_s[...]
        n_out[...] = n_s[...]
        m_out[...] = m_s[...]


def _mlstm(qkb, vb, gates_col, gates_row, c0, n0, m0, *, chunk, valid_rows):
    b, t, _ = qkb.shape
    nc = t // chunk
    hb = N_HEADS_B
    st4 = lambda shp: pl.BlockSpec((None,) + shp, lambda bb, ci: (bb, 0, 0, 0))
    kern = functools.partial(_mlstm_kernel, valid_rows=valid_rows)
    return pl.pallas_call(
        kern,
        out_shape=(jax.ShapeDtypeStruct((b, t, hb * DV_B), BF16),
                   jax.ShapeDtypeStruct(c0.shape, F32),
                   jax.ShapeDtypeStruct(n0.shape, F32),
                   jax.ShapeDtypeStruct(m0.shape, F32)),
        grid=(b, nc),
        in_specs=[pl.BlockSpec((None, chunk, qkb.shape[2]), lambda bb, ci: (bb, ci, 0)),
                  pl.BlockSpec((None, chunk, vb.shape[2]), lambda bb, ci: (bb, ci, 0)),
                  pl.BlockSpec((None, chunk, LANES), lambda bb, ci: (bb, ci, 0)),
                  pl.BlockSpec((None, 2 * hb, chunk), lambda bb, ci: (bb * nc + ci, 0, 0)),
                  st4((hb, DV_B, DK_B)), st4((hb, 1, DK_B)), st4((hb, 1, LANES))],
        out_specs=(pl.BlockSpec((None, chunk, hb * DV_B), lambda bb, ci: (bb, ci, 0)),
                   st4((hb, DV_B, DK_B)), st4((hb, 1, DK_B)), st4((hb, 1, LANES))),
        scratch_shapes=[pltpu.VMEM((hb, DV_B, DK_B), F32),
                        pltpu.VMEM((hb, 1, DK_B), F32),
                        pltpu.VMEM((hb, 1, LANES), F32)],
        compiler_params=_params(("parallel", "arbitrary")),
        name="mlstm",
    )(qkb, vb, gates_col, gates_row, c0, n0, m0)


def _ffn_kernel(x_ref, ya_ref, hb_ref, sga_ref, sgob_ref, gtm_ref, shf_ref, scf_ref, gtf_ref,
                gffn_ref, gfin_ref, wout_ref, wup_ref, wconv_ref, bconv_ref, wdown_ref, *rest, seq_rows):
    if seq_rows:
        fix1_ref, fix2_ref, y_ref, a_ref = rest
    else:
        y_ref, tail_ref, carry = rest

        @pl.when(pl.program_id(1) == 0)
        def _():
            carry[...] = jnp.zeros_like(carry)
    tm = x_ref.shape[0]
    dff = wdown_ref.shape[0]
    merged = (sga_ref[...].astype(F32) * ya_ref[...].astype(F32)
              + sgob_ref[...].astype(F32) * hb_ref[...].astype(F32))
    x1 = x_ref[...] + gtm_ref[...] * _dot(merged.astype(BF16), wout_ref[...])
    ms = jnp.mean(x1 * x1, axis=-1, keepdims=True)
    xn = x1 * lax.rsqrt(ms + RMS_EPS) * gffn_ref[...]
    h = (xn * (1.0 + scf_ref[...]) + shf_ref[...]).astype(BF16)

    edges = list(range(0, dff, FFN_CHUNK)) + [dff]
    chunks = list(zip(edges[:-1], edges[1:]))

    def up(c0, c1):
        return _dot(h, wup_ref[:, c0:c1]), _dot(h, wup_ref[:, dff + c0:dff + c1])

    nxt = up(*chunks[0])
    f = None
    for ci, (c0, c1) in enumerate(chunks):
        a, gb = nxt
        if ci + 1 < len(chunks):
            nxt = up(*chunks[ci + 1])
        row = lax.broadcasted_iota(jnp.int32, a.shape, 0)
        r1 = pltpu.roll(a, 1, axis=0)
        r2 = pltpu.roll(a, 2, axis=0)
        if seq_rows:
            assert seq_rows & (seq_rows - 1) == 0
            t = row & (seq_rows - 1)
            a1 = jnp.where(t >= 1, r1, fix1_ref[:, c0:c1])
            a2 = jnp.where(t >= 2, r2, fix2_ref[:, c0:c1])
            a_ref[:, c0:c1] = a
        else:
            prev1 = carry[SUBLANES - 1:SUBLANES, c0:c1]
            prev2 = carry[SUBLANES - 2:SUBLANES - 1, c0:c1]
            a1 = jnp.where(row == 0, prev1, r1)
            a2 = jnp.where(row == 0, prev2, jnp.where(row == 1, prev1, r2))
            tail = a[tm - SUBLANES:, :]
            carry[:, c0:c1] = tail
            tail_ref[:, c0:c1] = tail
        ac = (bconv_ref[:, c0:c1] + a2 * wconv_ref[0:1, c0:c1] + a1 * wconv_ref[1:2, c0:c1]
              + a * wconv_ref[2:3, c0:c1])
        gelu = 0.5 * ac * (1.0 + jnp.tanh(math.sqrt(2.0 / math.pi) * (ac + 0.044715 * (ac * ac * ac))))
        part = _dot((gelu * gb).astype(BF16), wdown_ref[c0:c1, :])
        f = part if f is None else f + part
    x2 = x1 + gtf_ref[...] * f
    ms2 = jnp.mean(x2 * x2, axis=-1, keepdims=True)
    y_ref[...] = x2 * lax.rsqrt(ms2 + RMS_EPS) * gfin_ref[...]


def _ffn(x, ya, hb, sga, sgob, gtm, shf, scf, gtf, g_ffn, g_fin, w_out, w_up, w_conv, b_conv, w_down,
         *, batch, seq_rows=0, fix1=None, fix2=None):
    n, d = x.shape
    dff = w_down.shape[0]
    tm = ROW_TILE
    if seq_rows:
        assert n == tm
        grid = (1, 1)
        mod_spec = pl.BlockSpec((tm, d), lambda bb, i: (0, 0))
        nt = 1
    else:
        nt = n // batch // tm
        grid = (batch, nt)
        mod_spec = pl.BlockSpec((None, 1, d), lambda bb, i: (bb, 0, 0))
    row = lambda w: pl.BlockSpec((tm, w), lambda bb, i: (bb * nt + i, 0))
    in_specs = [row(d), row(d), row(d), row(d), row(d), mod_spec, mod_spec, mod_spec, mod_spec,
                _resident((1, d)), _resident((1, d)), _resident(w_out.shape), _resident(w_up.shape),
                _resident(w_conv.shape), _resident((1, dff)), _resident(w_down.shape)]
    args = [x, ya, hb, sga, sgob, gtm, shf, scf, gtf, g_ffn, g_fin, w_out, w_up, w_conv, b_conv, w_down]
    if seq_rows:
        in_specs += [row(dff), row(dff)]
        args += [fix1, fix2]
        out_shape = (jax.ShapeDtypeStruct((n, d), F32), jax.ShapeDtypeStruct((n, dff), F32))
        out_specs = (row(d), row(dff))
        scratch = []
    else:
        out_shape = (jax.ShapeDtypeStruct((n, d), F32), jax.ShapeDtypeStruct((batch, SUBLANES, dff), F32))
        out_specs = (row(d), pl.BlockSpec((None, SUBLANES, dff), lambda bb, i: (bb, 0, 0)))
        scratch = [pltpu.VMEM((SUBLANES, dff), F32)]
    return pl.pallas_call(
        functools.partial(_ffn_kernel, seq_rows=seq_rows),
        out_shape=out_shape,
        grid=grid,
        in_specs=in_specs,
        out_specs=out_specs,
        scratch_shapes=scratch,
        compiler_params=_params(("parallel", "arbitrary")),
        name="merge_ffn",
    )(*args)


def _select_kernel(q_ref, km_ref, o_ref):
    nseq, ts, _ = q_ref.shape
    n_blocks = km_ref.shape[1]
    lane = lax.broadcasted_iota(jnp.int32, (ts, LANES), 1)
    for sq in range(nseq):
        for hh in range(N_HEADS_A):
            sl = slice(hh * HEAD_DIM_A, (hh + 1) * HEAD_DIM_A)
            g = _gate_scores(q_ref[sq, :, sl].astype(BF16), km_ref[sq, :, hh, :])
            blk = lax.broadcasted_iota(jnp.int32, g.shape, 1).astype(F32)
            out = jnp.zeros((ts, LANES), F32)
            for r in range(MOBA_TOPK):
                mx = jnp.max(g, axis=1, keepdims=True)
                idx = jnp.min(jnp.where(g == mx, blk, float(n_blocks)), axis=1, keepdims=True)
                out = jnp.where(lane == r, idx, out)
                g = jnp.where(blk == idx, -jnp.inf, g)
            o_ref[sq, hh] = out.astype(jnp.int32)


def _select_blocks(q, kmean):
    b, ts, w = q.shape
    _, nb, nh, hd = kmean.shape
    nseq = math.gcd(b, 4)
    return pl.pallas_call(
        _select_kernel,
        out_shape=jax.ShapeDtypeStruct((b, N_HEADS_A, ts, LANES), jnp.int32),
        grid=(b // nseq,),
        in_specs=[pl.BlockSpec((nseq, ts, w), lambda bb: (bb, 0, 0)),
                  pl.BlockSpec((nseq, nb, nh, hd), lambda bb: (bb, 0, 0, 0))],
        out_specs=pl.BlockSpec((nseq, N_HEADS_A, ts, LANES), lambda bb: (bb, 0, 0, 0)),
        compiler_params=_params(("parallel",)),
        name="select_blocks",
    )(q, kmean)


def _moba_sample_kernel(sel_ref, pt_ref, q_ref, kn_ref, vn_ref, ck_ref, cv_ref, o_ref, kbuf, vbuf, sem,
                        *, n_pages, per_blk):
    b = pl.program_id(0)
    hh = pl.program_id(1)
    ts = q_ref.shape[0]
    page = kbuf.shape[1]
    nsel = MOBA_TOPK * per_blk

    step = b * N_HEADS_A + hh
    nsteps = pl.num_programs(0) * N_HEADS_A
    nslots = ts * nsel

    def start_all(stp):
        head = stp % N_HEADS_A
        half = stp % 2
        for j in range(nslots):
            blk = sel_ref[stp * (ts * MOBA_TOPK) + j // per_blk]
            pg = pt_ref[(stp // N_HEADS_A) * n_pages + blk * per_blk + j % per_blk]
            pltpu.make_async_copy(ck_ref.at[pg, :, head, :], kbuf.at[half * nslots + j], sem.at[0, half]).start()
            pltpu.make_async_copy(cv_ref.at[pg, :, head, :], vbuf.at[half * nslots + j], sem.at[1, half]).start()

    @pl.when(step == 0)
    def _():
        start_all(step)

    @pl.when(step + 1 < nsteps)
    def _():
        start_all(step + 1)

    base = (step % 2) * nslots
    for buf, kind in ((kbuf, 0), (vbuf, 1)):
        whole = buf.at[pl.ds(base, nslots)]
        pltpu.make_async_copy(whole, whole, sem.at[kind, step % 2]).wait()

    qf = q_ref[...]
    q = qf.astype(BF16)
    kn = kn_ref[...]
    vn = vn_ref[...]
    jrow = lax.broadcasted_iota(jnp.int32, (ts, LANES), 0)
    for t in range(ts):
        qrep = jnp.broadcast_to(q[t:t + 1, :], (LANES, q.shape[1]))
        kc = kbuf[pl.ds(base + t * nsel, nsel)].reshape(nsel * page, -1).astype(BF16)
        vc = vbuf[pl.ds(base + t * nsel, nsel)].reshape(nsel * page, -1)
        s = _dot_nt(kc, qrep)
        s_own = jnp.where(jrow <= t, jnp.sum(kn * qf[t:t + 1, :], axis=1, keepdims=True), MASKED)
        m = jnp.maximum(jnp.max(s, axis=0, keepdims=True), jnp.max(s_own, axis=0, keepdims=True))
        p = jnp.exp2(s - m)
        p_own = jnp.exp2(s_own - m)
        l = jnp.sum(p, axis=0, keepdims=True) + jnp.sum(p_own, axis=0, keepdims=True)
        o = jnp.sum(p * vc, axis=0, keepdims=True) + jnp.sum(p_own * vn, axis=0, keepdims=True)
        o_ref[t:t + 1, :] = o / l


def _moba_sample(q, k_new, v_new, cache_k, cache_v, page_table, sel):
    b, n_pages = page_table.shape
    ts = q.shape[0] // b
    _, page, nh, hd = cache_k.shape
    per_blk = MOBA_BLOCK // page
    nslots = ts * MOBA_TOPK * per_blk
    blk = pl.BlockSpec((ts, hd), lambda bb, hh, *_: (bb, hh))
    kern = functools.partial(_moba_sample_kernel, n_pages=n_pages, per_blk=per_blk)
    return pl.pallas_call(
        kern,
        out_shape=jax.ShapeDtypeStruct(q.shape, F32),
        grid_spec=pltpu.PrefetchScalarGridSpec(
            num_scalar_prefetch=2,
            grid=(b, nh),
            in_specs=[blk, blk, blk, pl.BlockSpec(memory_space=pl.ANY), pl.BlockSpec(memory_space=pl.ANY)],
            out_specs=blk,
            scratch_shapes=[pltpu.VMEM((2 * nslots, page, hd), F32), pltpu.VMEM((2 * nslots, page, hd), F32),
                            pltpu.SemaphoreType.DMA((2, 2))]),
        compiler_params=_params(("arbitrary", "arbitrary")),
        name="moba_sample",
    )(sel.reshape(-1), page_table.reshape(-1), q, k_new, v_new, cache_k, cache_v)


def _rope_tables(pos):
    half = HEAD_DIM_A // 2
    inv = ROPE_THETA ** (-jnp.arange(half, dtype=F32) / half)
    ang = pos.astype(F32)[:, None] * inv[None, :]
    cos, sin = jnp.cos(ang), jnp.sin(ang)
    return jnp.concatenate([cos, cos], axis=1), jnp.concatenate([-sin, sin], axis=1)


def _pack_w_in(w_in):
    d = w_in.shape[0]
    a = 3 * N_HEADS_A * HEAD_DIM_A + 2 * N_HEADS_B * DK_B + 2 * N_HEADS_B * DV_B
    ng = 2 * N_HEADS_B
    w_main = (w_in[:, :a].astype(BF16), w_in[:, a + ng:].astype(BF16))
    w_gate = jnp.zeros((d, LANES), BF16).at[:, :ng].set(w_in[:, a:a + ng].astype(BF16))
    return w_main, w_gate


def _layer(x, c_mod, pos, w, *, cache=None, sample=None):
    (g_mix, w_main, w_gate, b_gate, w_out, g_ffn, w_up, w_conv, b_conv, w_down, g_fin) = w
    b, t, d = x.shape
    n = b * t
    wa = N_HEADS_A * HEAD_DIM_A
    dff = w_down.shape[0]
    hb = N_HEADS_B
    sh_m, sc_m, gt_m, sh_f, sc_f, gt_f = jnp.split(c_mod, 6, axis=-1)
    cos2, sin2 = _rope_tables(pos)
    x2d = x.reshape(n, d)

    if sample is None:
        per_seq = lambda a: a.reshape(b, 1, d)
        q, kf, kb16, vf, vb16, qkb, vbb, sga, sgob, gcol, grow, kmean = _inproj(
            x2d, g_mix, per_seq(sh_m), per_seq(sc_m), cos2, sin2, w_main, w_gate, b_gate,
            rows_per_mod=t, rows_per_pos=t)
        kmean = jnp.pad(kmean.reshape(b, t // MOBA_BLOCK, wa), ((0, 0), (0, LANES - t // MOBA_BLOCK), (0, 0)))
        ya, extra = _moba_prompt(q.reshape(b, t, wa), kb16.reshape(b, t, wa), vb16.reshape(b, t, wa), kmean, *cache)
        ya = ya.reshape(n, wa)
        chunk = MLSTM_CHUNK
        nc = t // chunk
        grow3 = grow.reshape(2 * hb, b * nc, chunk).transpose(1, 0, 2)
        c0 = jnp.zeros((b, hb, DV_B, DK_B), F32)
        n0 = jnp.zeros((b, hb, 1, DK_B), F32)
        m0 = jnp.zeros((b, hb, 1, LANES), F32)
        hbo, c_new, n_new, m_new = _mlstm(qkb.reshape(b, t, -1), vbb.reshape(b, t, -1), gcol.reshape(b, t, LANES),
                                          grow3, c0, n0, m0, chunk=chunk, valid_rows=chunk)
        y, tail = _ffn(x2d, ya, hbo.reshape(n, -1), sga, sgob, per_seq(gt_m), per_seq(sh_f), per_seq(sc_f),
                       per_seq(gt_f), g_ffn, g_fin, w_out, w_up, w_conv, b_conv, w_down, batch=b)
        conv_new = tail[:, SUBLANES - (CONV_W - 1):, :]
    else:
        cache_k, cache_v, page_table, kmean, c_st, n_st, m_st, conv0 = sample
        extra = None
        assert n == ROW_TILE and t >= CONV_W - 1
        past = page_table.shape[1] * cache_k.shape[1]
        assert past % MOBA_BLOCK == 0 and t <= MOBA_BLOCK
        per_row = lambda a: jnp.repeat(a, t, axis=0)
        cos2, sin2 = jnp.tile(cos2, (b, 1)), jnp.tile(sin2, (b, 1))
        q, kf, kb16, vf, vb16, qkb, vbb, sga, sgob, gcol, grow, _ = _inproj(
            x2d, g_mix, per_row(sh_m), per_row(sc_m), cos2, sin2, w_main, w_gate, b_gate,
            rows_per_mod=0, rows_per_pos=n)
        qf = q.astype(F32)
        sel = _select_blocks(qf.reshape(b, t, wa), kmean)[..., :MOBA_TOPK]
        ya = _moba_sample(qf, kf, vf, cache_k, cache_v, page_table, sel).astype(BF16)
        chunk = SAMPLE_CHUNK
        pad = lambda a: jnp.pad(a.reshape(b, t, -1), ((0, 0), (0, chunk - t), (0, 0)))
        grow3 = jnp.pad(grow.reshape(2 * hb, b, t).transpose(1, 0, 2), ((0, 0), (0, 0), (0, chunk - t)))
        m0 = jnp.broadcast_to(m_st[:, :, None, None], (b, hb, 1, LANES))
        hbo, c_new, n_new, m_new = _mlstm(pad(qkb), pad(vbb), pad(gcol), grow3, c_st, n_st[:, :, None, :], m0,
                                          chunk=chunk, valid_rows=t)
        hbo = hbo[:, :t].reshape(n, -1)
        zero = jnp.zeros((b, t - 2, dff), F32)
        fix1 = jnp.concatenate([conv0[:, 1:2], zero, zero[:, :1]], axis=1).reshape(n, dff)
        fix2 = jnp.concatenate([conv0, zero], axis=1).reshape(n, dff)
        y, a_full = _ffn(x2d, ya, hbo, sga, sgob, per_row(gt_m), per_row(sh_f), per_row(sc_f), per_row(gt_f),
                         g_ffn, g_fin, w_out, w_up, w_conv, b_conv, w_down, batch=b, seq_rows=t,
                         fix1=fix1, fix2=fix2)
        conv_new = a_full.reshape(b, t, dff)[:, t - (CONV_W - 1):]
    return (y.reshape(b, t, d), kf.reshape(b, t, N_HEADS_A, HEAD_DIM_A), vf.reshape(b, t, N_HEADS_A, HEAD_DIM_A),
            c_new, n_new[:, :, 0, :], m_new[:, :, 0, 0], conv_new, extra)


def kernel(x_prompt, x_sample, cache_k, cache_v, page_table, state_C, state_n, state_m, state_conv,
           c_prompt, c_sample, w_ada, b_ada, g_norm_mix, w_in, b_gates, w_out, g_norm_ffn,
           w_up, w_conv, b_conv, w_down, g_norm_final):
    depth = w_in.shape[0]
    assert depth == 1
    bp, tp, d = x_prompt.shape
    bs, ts, _ = x_sample.shape
    past = page_table.shape[1] * cache_k.shape[2]
    dl = 0
    mod = _ada(jnp.concatenate([c_prompt, c_sample], axis=0), w_ada[dl], b_ada[dl])
    w_main, w_gate = _pack_w_in(w_in[dl])
    b_gate = jnp.zeros((1, LANES), F32).at[0, :2 * N_HEADS_B].set(b_gates[dl])
    w = (g_norm_mix[dl].reshape(1, d), w_main, w_gate, b_gate, w_out[dl].astype(BF16),
         g_norm_ffn[dl].reshape(1, d), w_up[dl].astype(BF16), w_conv[dl], b_conv[dl].reshape(1, -1),
         w_down[dl].astype(BF16), g_norm_final.reshape(1, d))
    yp, kp, vp, cp, np_, mp, cvp, cache_kmean = _layer(
        x_prompt, mod[:bp], jnp.arange(tp, dtype=jnp.int32), w, cache=(cache_k[dl], page_table))
    ys, ks, vs, cs, ns, ms, cvs, _ = _layer(
        x_sample, mod[bp:], past + jnp.arange(ts, dtype=jnp.int32), w,
        sample=(cache_k[dl], cache_v[dl], page_table, cache_kmean,
                state_C[dl], state_n[dl], state_m[dl], state_conv[dl]))
    st = lambda a: a[None]
    return (yp, ys, st(kp), st(vp), st(ks), st(vs), st(cp), st(np_), st(mp), st(cs), st(ns), st(ms),
            st(cvp), st(cvs))
```

```python
import functools
import math

import jax
import jax.numpy as jnp
from jax import lax
from jax.experimental import pallas as pl
from jax.experimental.pallas import tpu as pltpu

F32 = jnp.float32
BF16 = jnp.bfloat16

N_HEADS_A = 8
HEAD_DIM_A = 128
MOBA_BLOCK = 256
MOBA_TOPK = 3
ROPE_THETA = 10000.0
N_HEADS_B = 4
DK_B = 128
DV_B = 256
CONV_W = 3
RMS_EPS = 1e-6
MASKED = -1e30
LANES = 128
SUBLANES = 8
ROW_TILE = 256
MLSTM_CHUNK = 256
SAMPLE_CHUNK = 128
FFN_CHUNK = 768
VMEM_LIMIT = 56 * 1024 * 1024


def _resident(shape):
    nd = len(shape)
    return pl.BlockSpec(shape, lambda *_: (0,) * nd, pipeline_mode=pl.Buffered(1))


def _params(sem):
    return pltpu.CompilerParams(dimension_semantics=sem, vmem_limit_bytes=VMEM_LIMIT)


def _split_bf16(x, pieces):
    out = []
    r = x
    for _ in range(pieces):
        p = r.astype(BF16)
        out.append(p)
        r = r - p.astype(F32)
    return out


def _dot_nt(a, b):
    return lax.dot_general(a, b, (((1,), (1,)), ((), ())), preferred_element_type=F32)


def _dot_tn(a, b):
    return lax.dot_general(a, b, (((0,), (0,)), ((), ())), preferred_element_type=F32)


def _dot(a, b):
    return jnp.dot(a, b, preferred_element_type=F32)


def _ada_kernel(c_ref, w_ref, b_ref, o_ref):
    c = c_ref[...]
    s = (c * jax.nn.sigmoid(c)).astype(BF16)
    o_ref[...] = _dot(s, w_ref[...].astype(BF16)) + b_ref[...]


def _ada(c, w_ada, b_ada):
    n, d = c.shape
    d_out = w_ada.shape[1]
    tn = d
    return pl.pallas_call(
        _ada_kernel,
        out_shape=jax.ShapeDtypeStruct((n, d_out), F32),
        grid=(d_out // tn,),
        in_specs=[pl.BlockSpec((n, d), lambda j: (0, 0)),
                  pl.BlockSpec((d, tn), lambda j: (0, j)),
                  pl.BlockSpec((1, tn), lambda j: (0, j))],
        out_specs=pl.BlockSpec((n, tn), lambda j: (0, j)),
        compiler_params=_params(("arbitrary",)),
        name="ada_mod",
    )(c, w_ada, b_ada.reshape(1, d_out))


def _inproj_kernel(x_ref, g_ref, sh_ref, sc_ref, cos_ref, sin_ref, wa_ref, wb_ref, wg_ref, bg_ref,
                   q_ref, kf_ref, kb_ref, vf_ref, vb_ref, qkb_ref, vbb_ref, sga_ref, sgob_ref,
                   gc_ref, gr_ref, km_ref):
    tm, d = x_ref.shape
    wa = N_HEADS_A * HEAD_DIM_A
    x = x_ref[...]
    ms = jnp.mean(x * x, axis=-1, keepdims=True)
    xn = x * lax.rsqrt(ms + RMS_EPS) * g_ref[...]
    h = (xn * (1.0 + sc_ref[...]) + sh_ref[...]).astype(BF16)
    cos = cos_ref[...]
    sin = sin_ref[...]

    n_first = wa_ref.shape[1] // wa

    def proj(c):
        if c < n_first:
            return _dot(h, wa_ref[:, c * wa:(c + 1) * wa])
        return _dot(h, wb_ref[:, (c - n_first) * wa:(c - n_first + 1) * wa])

    def rope(a, hh):
        xh = a[:, hh * HEAD_DIM_A:(hh + 1) * HEAD_DIM_A]
        return xh * cos + pltpu.roll(xh, HEAD_DIM_A // 2, axis=1) * sin

    qscale = HEAD_DIM_A ** -0.5 * math.log2(math.e)
    a = proj(0)
    for hh in range(N_HEADS_A):
        sl = slice(hh * HEAD_DIM_A, (hh + 1) * HEAD_DIM_A)
        q_ref[:, sl] = (rope(a, hh) * qscale).astype(BF16)
    a = proj(1)
    for hh in range(N_HEADS_A):
        sl = slice(hh * HEAD_DIM_A, (hh + 1) * HEAD_DIM_A)
        kr = rope(a, hh)
        kf_ref[:, sl] = kr
        kb_ref[:, sl] = kr.astype(BF16)
        for g in range(tm // MOBA_BLOCK):
            km_ref[g, :, sl] = jnp.mean(kr[g * MOBA_BLOCK:(g + 1) * MOBA_BLOCK], axis=0, keepdims=True)
    a = proj(2)
    vf_ref[...] = a
    vb_ref[...] = a.astype(BF16)
    a = proj(3)
    wqk = N_HEADS_B * DK_B
    qkb_ref[:, :wqk] = a[:, :wqk].astype(BF16)
    qkb_ref[:, wqk:] = (a[:, wqk:] * (DK_B ** -0.5)).astype(BF16)
    vbb_ref[...] = proj(4).astype(BF16)
    so = jax.nn.sigmoid(proj(5))
    sga_ref[...] = jax.nn.sigmoid(proj(6)).astype(BF16)
    sgob_ref[...] = (jax.nn.sigmoid(proj(7)) * so).astype(BF16)
    g = _dot(h, wg_ref[...]) + bg_ref[...]
    lane = lax.broadcasted_iota(jnp.int32, g.shape, 1)
    logsig = jnp.minimum(g, 0.0) - jnp.log(1.0 + jnp.exp(-jnp.abs(g)))
    g = jnp.where(lane >= N_HEADS_B, logsig, g)
    gc_ref[...] = g
    gr_ref[...] = g.T[:2 * N_HEADS_B, :]


def _inproj(x, g_mix, shift, scale, cos2, sin2, w_main, w_gate, b_gate, *, rows_per_mod, rows_per_pos):
    n, d = x.shape
    tm = ROW_TILE
    assert n % tm == 0 and tm % MOBA_BLOCK == 0
    wa = N_HEADS_A * HEAD_DIM_A
    if rows_per_mod:
        assert rows_per_mod % tm == 0
        mod_spec = pl.BlockSpec((None, 1, d), lambda i: (i // (rows_per_mod // tm), 0, 0))
    else:
        mod_spec = pl.BlockSpec((tm, d), lambda i: (i, 0))
    assert rows_per_pos % tm == 0
    pos_spec = pl.BlockSpec((tm, HEAD_DIM_A), lambda i: (i % (rows_per_pos // tm), 0))
    row = lambda w: pl.BlockSpec((tm, w), lambda i: (i, 0))
    nblk = tm // MOBA_BLOCK
    out_shape = (
        jax.ShapeDtypeStruct((n, wa), BF16),
        jax.ShapeDtypeStruct((n, wa), F32),
        jax.ShapeDtypeStruct((n, wa), BF16),
        jax.ShapeDtypeStruct((n, wa), F32),
        jax.ShapeDtypeStruct((n, wa), BF16),
        jax.ShapeDtypeStruct((n, 2 * N_HEADS_B * DK_B), BF16),
        jax.ShapeDtypeStruct((n, N_HEADS_B * DV_B), BF16),
        jax.ShapeDtypeStruct((n, wa), BF16),
        jax.ShapeDtypeStruct((n, N_HEADS_B * DV_B), BF16),
        jax.ShapeDtypeStruct((n, LANES), F32),
        jax.ShapeDtypeStruct((2 * N_HEADS_B, n), F32),
        jax.ShapeDtypeStruct((n // MOBA_BLOCK, 1, wa), F32),
    )
    out_specs = (row(wa), row(wa), row(wa), row(wa), row(wa), row(2 * N_HEADS_B * DK_B), row(N_HEADS_B * DV_B),
                 row(wa), row(N_HEADS_B * DV_B), row(LANES),
                 pl.BlockSpec((2 * N_HEADS_B, tm), lambda i: (0, i)),
                 pl.BlockSpec((nblk, 1, wa), lambda i: (i, 0, 0)))
    return pl.pallas_call(
        _inproj_kernel,
        out_shape=out_shape,
        grid=(n // tm,),
        in_specs=[row(d), _resident((1, d)), mod_spec, mod_spec, pos_spec, pos_spec,
                  _resident(w_main[0].shape), _resident(w_main[1].shape), _resident(w_gate.shape),
                  _resident((1, LANES))],
        out_specs=out_specs,
        compiler_params=_params(("parallel",)),
        name="in_proj",
    )(x, g_mix, shift, scale, cos2, sin2, *w_main, w_gate, b_gate)


def _topk_mask(gate, valid, axis):
    blk = lax.broadcasted_iota(jnp.int32, gate.shape, axis).astype(F32)
    g = jnp.where(valid, gate, -jnp.inf)
    sel = jnp.zeros(gate.shape, F32)
    for _ in range(MOBA_TOPK):
        mx = jnp.max(g, axis=axis, keepdims=True)
        idx = jnp.min(jnp.where(g == mx, blk, float(gate.shape[axis])), axis=axis, keepdims=True)
        pick = blk == idx
        sel = jnp.where(pick, jnp.where(valid, 1.0, 0.0), sel)
        g = jnp.where(pick, -jnp.inf, g)
    return sel


def _gate_scores(q, kmean):
    hi, lo = _split_bf16(kmean, 2)
    return _dot_nt(q, hi) + _dot_nt(q, lo)


def _gate_scores_t(q, kmean):
    hi, lo = _split_bf16(kmean, 2)
    return _dot_nt(hi, q) + _dot_nt(lo, q)


def _moba_prompt_kernel(pt_ref, q_ref, k_ref, v_ref, km_ref, ck_ref, o_ref, cm_ref,
                        kaug, vext, qa, s0, s1, mx0, mx1, m_ref, acc_ref, pbuf, psem, *, pages_per_step, tk):
    i = pl.program_id(2)
    tq, hd = q_ref.shape
    t = k_ref.shape[0]
    n_own = tq // tk
    step = (pl.program_id(0) * pl.num_programs(1) + pl.program_id(1)) * pl.num_programs(2) + i
    n_steps = pl.num_programs(0) * pl.num_programs(1) * pl.num_programs(2)

    def page_copy(stp, p):
        half = stp % 2
        return pltpu.make_async_copy(ck_ref.at[pt_ref[stp * pages_per_step + p]],
                                     pbuf.at[half * pages_per_step + p], psem.at[half])

    @pl.when(i == 0)
    def _():
        vext[:, :hd] = v_ref[...]
        vext[:, hd:] = jnp.ones((t, hd), BF16)
        rowi = lax.broadcasted_iota(jnp.int32, (hd, tq), 0)
        coli = lax.broadcasted_iota(jnp.int32, (hd, tq), 1)
        for j in range(t // tq):
            kaug[:hd, j * tq:(j + 1) * tq] = k_ref[j * tq:(j + 1) * tq, :].T
            blk_of_key = j * (tq // MOBA_BLOCK) + coli // MOBA_BLOCK
            kaug[hd:, j * tq:(j + 1) * tq] = jnp.where(rowi == blk_of_key, 1.0, 0.0).astype(BF16)

    @pl.when(step == 0)
    def _():
        for p in range(pages_per_step):
            page_copy(step, p).start()

    @pl.when(step + 1 < n_steps)
    def _():
        for p in range(pages_per_step):
            page_copy(step + 1, p).start()

    for p in range(pages_per_step):
        page_copy(step, p).wait()
    per_blk = MOBA_BLOCK // pbuf.shape[1]
    for r in range(pages_per_step // per_blk):
        base = (step % 2) * pages_per_step + r * per_blk
        tot = jnp.sum(pbuf[base], axis=0)
        for p in range(1, per_blk):
            tot = tot + jnp.sum(pbuf[base + p], axis=0)
        cm_ref[r] = tot * (1.0 / MOBA_BLOCK)

    q = q_ref[...]
    nbp = -(-(t // MOBA_BLOCK) // SUBLANES) * SUBLANES
    gate = _gate_scores_t(q, km_ref[:nbp, :])
    blk = lax.broadcasted_iota(jnp.int32, gate.shape, 0)
    own = i * (tq // MOBA_BLOCK) + lax.broadcasted_iota(jnp.int32, gate.shape, 1) // MOBA_BLOCK
    sel = _topk_mask(gate, blk < own, 0)
    bias = jnp.where(blk == own, 0.0, jnp.where(sel > 0.5, 0.0, MASKED))
    bias = jnp.concatenate([bias, jnp.zeros((LANES - nbp, tq), F32)], axis=0)
    qa[:, :hd] = q
    qa[:, hd:] = bias.T.astype(BF16)
    m_ref[...] = jnp.full(m_ref.shape, MASKED, F32)
    acc_ref[...] = jnp.zeros(acc_ref.shape, F32)

    n_lt = tk // LANES

    def lane_tile_max(s):
        pm = s[:, :LANES]
        for j in range(1, n_lt):
            pm = jnp.maximum(pm, s[:, j * LANES:(j + 1) * LANES])
        return pm

    def scores_to(slot, e, row0=0):
        s_ref, mx_ref = slot
        st = pl.multiple_of(e * tk, tk)
        s = _dot(qa[row0:, :], kaug[:, pl.ds(st, tk)])
        s_ref[row0:, :] = s
        mx_ref[row0:, :] = lane_tile_max(s)

    def absorb(slot, e, row0=0):
        s_ref, mx_ref = slot
        st = pl.multiple_of(e * tk, tk)
        m = m_ref[row0:, :]
        m_new = jnp.maximum(m, jnp.broadcast_to(jnp.max(mx_ref[row0:, :], axis=1, keepdims=True), m.shape))
        p = jnp.concatenate([jnp.exp2(s_ref[row0:, j * LANES:(j + 1) * LANES] - m_new).astype(BF16)
                             for j in range(n_lt)], axis=1)
        alpha = jnp.exp2(m - m_new)
        pv = _dot(p, vext[pl.ds(st, tk), :])
        acc_ref[row0:, :hd] = alpha * acc_ref[row0:, :hd] + pv[:, :hd]
        acc_ref[row0:, hd:] = alpha * acc_ref[row0:, hd:] + pv[:, hd:]
        m_ref[row0:, :] = m_new

    def absorb_own(slot, j):
        s_ref, mx_ref = slot
        rows = slice(j * tk, (j + 1) * tk)
        r = lax.broadcasted_iota(jnp.int32, (tk, tk), 0)
        c = lax.broadcasted_iota(jnp.int32, (tk, tk), 1)
        s = jnp.where(c <= r, s_ref[rows, :], MASKED)
        s_ref[rows, :] = s
        mx_ref[rows, :] = lane_tile_max(s)
        absorb(slot, i * n_own + j, row0=j * tk)

    slot0, slot1 = (s0, mx0), (s1, mx1)
    scores_to(slot0, 0)

    def pair(p, carry):
        e = 2 * p
        scores_to(slot1, e + 1)
        absorb(slot0, e)
        scores_to(slot0, e + 2)
        absorb(slot1, e + 1)
        return carry

    lax.fori_loop(0, i * (n_own // 2), pair, 0)

    slots = (slot0, slot1)
    for j in range(n_own):
        if j + 1 < n_own:
            scores_to(slots[(j + 1) % 2], i * n_own + j + 1, row0=(j + 1) * tk)
        absorb_own(slots[j % 2], j)
    o_ref[...] = (acc_ref[:, :hd] / acc_ref[:, hd:]).astype(o_ref.dtype)


def _moba_prompt(q, k, v, kmean, cache_k, page_table, *, blocks_per_key_tile=2, key_tiles_per_query_tile=2):
    b, t, w = q.shape
    tk = blocks_per_key_tile * MOBA_BLOCK
    tq = key_tiles_per_query_tile * tk
    hd = HEAD_DIM_A
    nh = w // hd
    assert key_tiles_per_query_tile % 2 == 0
    assert t % tq == 0 and t // MOBA_BLOCK <= LANES and kmean.shape[1] == LANES
    n_steps = b * nh * (t // tq)
    bs, n_pages = page_table.shape
    page = cache_k.shape[1]
    per_blk = MOBA_BLOCK // page
    pages_per_step = bs * n_pages // n_steps
    assert pages_per_step * n_steps == bs * n_pages and pages_per_step % per_blk == 0 and MOBA_BLOCK % page == 0
    blk_per_step = pages_per_step // per_blk

    def step_of(bb, hh, i):
        return (bb * nh + hh) * (t // tq) + i

    ya, cmean = pl.pallas_call(
        functools.partial(_moba_prompt_kernel, pages_per_step=pages_per_step, tk=tk),
        out_shape=(jax.ShapeDtypeStruct((b, t, w), BF16),
                   jax.ShapeDtypeStruct((n_steps, blk_per_step) + cache_k.shape[2:], F32)),
        grid_spec=pltpu.PrefetchScalarGridSpec(
            num_scalar_prefetch=1,
            grid=(b, nh, t // tq),
            in_specs=[pl.BlockSpec((None, tq, hd), lambda bb, hh, i, pt: (bb, i, hh)),
                      pl.BlockSpec((None, t, hd), lambda bb, hh, i, pt: (bb, 0, hh)),
                      pl.BlockSpec((None, t, hd), lambda bb, hh, i, pt: (bb, 0, hh)),
                      pl.BlockSpec((None, LANES, hd), lambda bb, hh, i, pt: (bb, 0, hh)),
                      pl.BlockSpec(memory_space=pl.ANY)],
            out_specs=(pl.BlockSpec((None, tq, hd), lambda bb, hh, i, pt: (bb, i, hh)),
                       pl.BlockSpec((None, blk_per_step) + cache_k.shape[2:],
                                    lambda bb, hh, i, pt: (step_of(bb, hh, i), 0, 0, 0))),
            scratch_shapes=[pltpu.VMEM((2 * hd, t), BF16), pltpu.VMEM((t, 2 * hd), BF16),
                            pltpu.VMEM((tq, 2 * hd), BF16),
                            pltpu.VMEM((tq, tk), F32), pltpu.VMEM((tq, tk), F32),
                            pltpu.VMEM((tq, LANES), F32), pltpu.VMEM((tq, LANES), F32),
                            pltpu.VMEM((tq, LANES), F32), pltpu.VMEM((tq, 2 * hd), F32),
                            pltpu.VMEM((2 * pages_per_step,) + cache_k.shape[1:], F32),
                            pltpu.SemaphoreType.DMA((2,))]),
        compiler_params=_params(("arbitrary", "arbitrary", "arbitrary")),
        name="moba_prompt",
    )(page_table.reshape(-1), q, k, v, kmean, cache_k)
    return ya, cmean.reshape((bs, n_pages // per_blk) + cache_k.shape[2:])


def _mlstm_kernel(qk_ref, v_ref, gc_ref, gr_ref, c0_ref, n0_ref, m0_ref,
                  h_ref, c_out, n_out, m_out, c_s, n_s, m_s, *, valid_rows):
    ci = pl.program_id(1)
    L = qk_ref.shape[0]
    wqk = N_HEADS_B * DK_B

    @pl.when(ci == 0)
    def _():
        c_s[...] = c0_ref[...]
        n_s[...] = n0_ref[...]
        m_s[...] = m0_ref[...]

    gc = gc_ref[...]
    gr = gr_ref[...]
    if valid_rows < L:
        rr = lax.broadcasted_iota(jnp.int32, gc.shape, 0)
        ln = lax.broadcasted_iota(jnp.int32, gc.shape, 1)
        gc = jnp.where(rr < valid_rows, gc, jnp.where(ln < N_HEADS_B, MASKED, 0.0))
        cc = lax.broadcasted_iota(jnp.int32, gr.shape, 1)
        rw = lax.broadcasted_iota(jnp.int32, gr.shape, 0)
        gr = jnp.where(cc < valid_rows, gr, jnp.where(rw < N_HEADS_B, MASKED, 0.0))

    r = lax.broadcasted_iota(jnp.int32, (L, L), 0)
    c = lax.broadcasted_iota(jnp.int32, (L, L), 1)
    causal = c <= r
    tri = jnp.where(causal, 1.0, 0.0).astype(BF16)
    triu = jnp.where(r <= c, 1.0, 0.0).astype(BF16)
    bc_all = sum(_dot(tri, p) for p in _split_bf16(gc, 3))
    br_all = sum(_dot(p, triu) for p in _split_bf16(gr, 3))

    def rep(x, width):
        return jnp.concatenate([x] * (width // LANES), axis=1)

    ones = jnp.ones((L, LANES), BF16)
    pre = []
    for hh in range(N_HEADS_B):
        q = qk_ref[:, hh * DK_B:(hh + 1) * DK_B]
        k = qk_ref[:, wqk + hh * DK_B:wqk + (hh + 1) * DK_B]
        cn = jnp.concatenate([c_s[hh], jnp.broadcast_to(n_s[hh], (LANES, DK_B))], axis=0).astype(BF16)
        pre.append((_dot_nt(q, k), _dot_nt(q, cn)))
    for hh in range(N_HEADS_B):
        k = qk_ref[:, wqk + hh * DK_B:wqk + (hh + 1) * DK_B]
        s_qk, qc = pre[hh]
        v1 = jnp.concatenate([v_ref[:, hh * DV_B:(hh + 1) * DV_B], ones], axis=1)
        i_col = jnp.broadcast_to(gc[:, hh:hh + 1], (L, LANES))
        b_col = jnp.broadcast_to(bc_all[:, N_HEADS_B + hh:N_HEADS_B + hh + 1], (L, LANES))
        i_row = gr[hh:hh + 1, :]
        b_row = br_all[N_HEADS_B + hh:N_HEADS_B + hh + 1, :]
        C = c_s[hh]
        n = n_s[hh]
        m_prev = m_s[hh]

        dmat = jnp.where(causal, rep(b_col, L) - b_row + i_row, MASKED)
        inter = b_col + m_prev
        m_t = jnp.maximum(inter, jnp.broadcast_to(jnp.max(dmat, axis=1, keepdims=True), (L, LANES)))
        w = jnp.exp(dmat - rep(m_t, L))
        a_inter = jnp.exp(inter - m_t)
        qk = (s_qk * w).astype(BF16)
        qv = _dot(qk, v1)
        num = rep(a_inter, DV_B) * qc[:, :DV_B] + qv[:, :DV_B]
        den = a_inter * qc[:, DV_B:] + qv[:, DV_B:]
        inv = 1.0 / jnp.maximum(jnp.abs(den), jnp.exp(-m_t))
        h_ref[:, hh * DV_B:(hh + 1) * DV_B] = (num * rep(inv, DV_B)).astype(h_ref.dtype)

        b_last = b_col[L - 1:L, :]
        g_col = b_last - b_col + i_col
        m_new = jnp.maximum(b_last + m_prev, jnp.max(g_col, axis=0, keepdims=True))
        decay = jnp.exp(b_last + m_prev - m_new)
        kw = k.astype(F32) * jnp.exp(g_col - m_new)
        c_s[hh] = decay * C + _dot_tn(v1[:, :DV_B], kw.astype(BF16))
        n_s[hh] = decay * n + jnp.sum(kw, axis=0, keepdims=True)
        m_s[hh] = m_new

    @pl.when(ci == pl.num_programs(1) - 1)
    def _():
        c_out[...] = c_s[...]
        n_out[...] = n_s[...]
        m_out[...] = m_s[...]


def _mlstm(qkb, vb, gates_col, gates_row, c0, n0, m0, *, chunk, valid_rows):
    b, t, _ = qkb.shape
    nc = t // chunk
    hb = N_HEADS_B
    st4 = lambda shp: pl.BlockSpec((None,) + shp, lambda bb, ci: (bb, 0, 0, 0))
    kern = functools.partial(_mlstm_kernel, valid_rows=valid_rows)
    return pl.pallas_call(
        kern,
        out_shape=(jax.ShapeDtypeStruct((b, t, hb * DV_B), BF16),
                   jax.ShapeDtypeStruct(c0.shape, F32),
                   jax.ShapeDtypeStruct(n0.shape, F32),
                   jax.ShapeDtypeStruct(m0.shape, F32)),
        grid=(b, nc),
        in_specs=[pl.BlockSpec((None, chunk, qkb.shape[2]), lambda bb, ci: (bb, ci, 0)),
                  pl.BlockSpec((None, chunk, vb.shape[2]), lambda bb, ci: (bb, ci, 0)),
                  pl.BlockSpec((None, chunk, LANES), lambda bb, ci: (bb, ci, 0)),
                  pl.BlockSpec((None, 2 * hb, chunk), lambda bb, ci: (bb * nc + ci, 0, 0)),
                  st4((hb, DV_B, DK_B)), st4((hb, 1, DK_B)), st4((hb, 1, LANES))],
        out_specs=(pl.BlockSpec((None, chunk, hb * DV_B), lambda bb, ci: (bb, ci, 0)),
                   st4((hb, DV_B, DK_B)), st4((hb, 1, DK_B)), st4((hb, 1, LANES))),
        scratch_shapes=[pltpu.VMEM((hb, DV_B, DK_B), F32),
                        pltpu.VMEM((hb, 1, DK_B), F32),
                        pltpu.VMEM((hb, 1, LANES), F32)],
        compiler_params=_params(("parallel", "arbitrary")),
        name="mlstm",
    )(qkb, vb, gates_col, gates_row, c0, n0, m0)


def _ffn_kernel(x_ref, ya_ref, hb_ref, sga_ref, sgob_ref, gtm_ref, shf_ref, scf_ref, gtf_ref,
                gffn_ref, gfin_ref, wout_ref, wup_ref, wconv_ref, bconv_ref, wdown_ref, *rest, seq_rows):
    if seq_rows:
        fix1_ref, fix2_ref, y_ref, a_ref = rest
    else:
        y_ref, tail_ref, carry = rest

        @pl.when(pl.program_id(1) == 0)
        def _():
            carry[...] = jnp.zeros_like(carry)
    tm = x_ref.shape[0]
    dff = wdown_ref.shape[0]
    merged = (sga_ref[...].astype(F32) * ya_ref[...].astype(F32)
              + sgob_ref[...].astype(F32) * hb_ref[...].astype(F32))
    x1 = x_ref[...] + gtm_ref[...] * _dot(merged.astype(BF16), wout_ref[...])
    ms = jnp.mean(x1 * x1, axis=-1, keepdims=True)
    xn = x1 * lax.rsqrt(ms + RMS_EPS) * gffn_ref[...]
    h = (xn * (1.0 + scf_ref[...]) + shf_ref[...]).astype(BF16)

    edges = list(range(0, dff, FFN_CHUNK)) + [dff]
    chunks = list(zip(edges[:-1], edges[1:]))

    def up(c0, c1):
        return _dot(h, wup_ref[:, c0:c1]), _dot(h, wup_ref[:, dff + c0:dff + c1])

    nxt = up(*chunks[0])
    f = None
    for ci, (c0, c1) in enumerate(chunks):
        a, gb = nxt
        if ci + 1 < len(chunks):
            nxt = up(*chunks[ci + 1])
        row = lax.broadcasted_iota(jnp.int32, a.shape, 0)
        r1 = pltpu.roll(a, 1, axis=0)
        r2 = pltpu.roll(a, 2, axis=0)
        if seq_rows:
            assert seq_rows & (seq_rows - 1) == 0
            t = row & (seq_rows - 1)
            a1 = jnp.where(t >= 1, r1, fix1_ref[:, c0:c1])
            a2 = jnp.where(t >= 2, r2, fix2_ref[:, c0:c1])
            a_ref[:, c0:c1] = a
        else:
            prev1 = carry[SUBLANES - 1:SUBLANES, c0:c1]
            prev2 = carry[SUBLANES - 2:SUBLANES - 1, c0:c1]
            a1 = jnp.where(row == 0, prev1, r1)
            a2 = jnp.where(row == 0, prev2, jnp.where(row == 1, prev1, r2))
            tail = a[tm - SUBLANES:, :]
            carry[:, c0:c1] = tail
            tail_ref[:, c0:c1] = tail
        ac = (bconv_ref[:, c0:c1] + a2 * wconv_ref[0:1, c0:c1] + a1 * wconv_ref[1:2, c0:c1]
              + a * wconv_ref[2:3, c0:c1])
        gelu = 0.5 * ac * (1.0 + jnp.tanh(math.sqrt(2.0 / math.pi) * (ac + 0.044715 * (ac * ac * ac))))
        part = _dot((gelu * gb).astype(BF16), wdown_ref[c0:c1, :])
        f = part if f is None else f + part
    x2 = x1 + gtf_ref[...] * f
    ms2 = jnp.mean(x2 * x2, axis=-1, keepdims=True)
    y_ref[...] = x2 * lax.rsqrt(ms2 + RMS_EPS) * gfin_ref[...]


def _ffn(x, ya, hb, sga, sgob, gtm, shf, scf, gtf, g_ffn, g_fin, w_out, w_up, w_conv, b_conv, w_down,
         *, batch, seq_rows=0, fix1=None, fix2=None):
    n, d = x.shape
    dff = w_down.shape[0]
    tm = ROW_TILE
    if seq_rows:
        assert n == tm
        grid = (1, 1)
        mod_spec = pl.BlockSpec((tm, d), lambda bb, i: (0, 0))
        nt = 1
    else:
        nt = n // batch // tm
        grid = (batch, nt)
        mod_spec = pl.BlockSpec((None, 1, d), lambda bb, i: (bb, 0, 0))
    row = lambda w: pl.BlockSpec((tm, w), lambda bb, i: (bb * nt + i, 0))
    in_specs = [row(d), row(d), row(d), row(d), row(d), mod_spec, mod_spec, mod_spec, mod_spec,
                _resident((1, d)), _resident((1, d)), _resident(w_out.shape), _resident(w_up.shape),
                _resident(w_conv.shape), _resident((1, dff)), _resident(w_down.shape)]
    args = [x, ya, hb, sga, sgob, gtm, shf, scf, gtf, g_ffn, g_fin, w_out, w_up, w_conv, b_conv, w_down]
    if seq_rows:
        in_specs += [row(dff), row(dff)]
        args += [fix1, fix2]
        out_shape = (jax.ShapeDtypeStruct((n, d), F32), jax.ShapeDtypeStruct((n, dff), F32))
        out_specs = (row(d), row(dff))
        scratch = []
    else:
        out_shape = (jax.ShapeDtypeStruct((n, d), F32), jax.ShapeDtypeStruct((batch, SUBLANES, dff), F32))
        out_specs = (row(d), pl.BlockSpec((None, SUBLANES, dff), lambda bb, i: (bb, 0, 0)))
        scratch = [pltpu.VMEM((SUBLANES, dff), F32)]
    return pl.pallas_call(
        functools.partial(_ffn_kernel, seq_rows=seq_rows),
        out_shape=out_shape,
        grid=grid,
        in_specs=in_specs,
        out_specs=out_specs,
        scratch_shapes=scratch,
        compiler_params=_params(("parallel", "arbitrary")),
        name="merge_ffn",
    )(*args)


def _select_kernel(q_ref, km_ref, o_ref):
    nseq, ts, _ = q_ref.shape
    n_blocks = km_ref.shape[1]
    lane = lax.broadcasted_iota(jnp.int32, (ts, LANES), 1)
    for sq in range(nseq):
        for hh in range(N_HEADS_A):
            sl = slice(hh * HEAD_DIM_A, (hh + 1) * HEAD_DIM_A)
            g = _gate_scores(q_ref[sq, :, sl].astype(BF16), km_ref[sq, :, hh, :])
            blk = lax.broadcasted_iota(jnp.int32, g.shape, 1).astype(F32)
            out = jnp.zeros((ts, LANES), F32)
            for r in range(MOBA_TOPK):
                mx = jnp.max(g, axis=1, keepdims=True)
                idx = jnp.min(jnp.where(g == mx, blk, float(n_blocks)), axis=1, keepdims=True)
                out = jnp.where(lane == r, idx, out)
                g = jnp.where(blk == idx, -jnp.inf, g)
            o_ref[sq, hh] = out.astype(jnp.int32)


def _select_blocks(q, kmean):
    b, ts, w = q.shape
    _, nb, nh, hd = kmean.shape
    nseq = math.gcd(b, 4)
    return pl.pallas_call(
        _select_kernel,
        out_shape=jax.ShapeDtypeStruct((b, N_HEADS_A, ts, LANES), jnp.int32),
        grid=(b // nseq,),
        in_specs=[pl.BlockSpec((nseq, ts, w), lambda bb: (bb, 0, 0)),
                  pl.BlockSpec((nseq, nb, nh, hd), lambda bb: (bb, 0, 0, 0))],
        out_specs=pl.BlockSpec((nseq, N_HEADS_A, ts, LANES), lambda bb: (bb, 0, 0, 0)),
        compiler_params=_params(("parallel",)),
        name="select_blocks",
    )(q, kmean)


def _moba_sample_kernel(sel_ref, pt_ref, q_ref, kn_ref, vn_ref, ck_ref, cv_ref, o_ref, kbuf, vbuf, sem,
                        *, n_pages, per_blk):
    b = pl.program_id(0)
    hh = pl.program_id(1)
    ts = q_ref.shape[0]
    page = kbuf.shape[1]
    nsel = MOBA_TOPK * per_blk

    step = b * N_HEADS_A + hh
    nsteps = pl.num_programs(0) * N_HEADS_A
    nslots = ts * nsel

    def start_all(stp):
        head = stp % N_HEADS_A
        half = stp % 2
        for j in range(nslots):
            blk = sel_ref[stp * (ts * MOBA_TOPK) + j // per_blk]
            pg = pt_ref[(stp // N_HEADS_A) * n_pages + blk * per_blk + j % per_blk]
            pltpu.make_async_copy(ck_ref.at[pg, :, head, :], kbuf.at[half * nslots + j],
                                  sem.at[0, half]).start(priority=j % 2)
            pltpu.make_async_copy(cv_ref.at[pg, :, head, :], vbuf.at[half * nslots + j],
                                  sem.at[1, half]).start(priority=(j + 1) % 2)

    @pl.when(step == 0)
    def _():
        start_all(step)

    @pl.when(step + 1 < nsteps)
    def _():
        start_all(step + 1)

    base = (step % 2) * nslots
    for buf, kind in ((kbuf, 0), (vbuf, 1)):
        whole = buf.at[pl.ds(base, nslots)]
        pltpu.make_async_copy(whole, whole, sem.at[kind, step % 2]).wait()

    qf = q_ref[...]
    q = qf.astype(BF16)
    kn = kn_ref[...]
    vn = vn_ref[...]
    jrow = lax.broadcasted_iota(jnp.int32, (ts, LANES), 0)
    for t in range(ts):
        qrep = jnp.broadcast_to(q[t:t + 1, :], (LANES, q.shape[1]))
        kc = kbuf[pl.ds(base + t * nsel, nsel)].reshape(nsel * page, -1).astype(BF16)
        vc = vbuf[pl.ds(base + t * nsel, nsel)].reshape(nsel * page, -1)
        s = _dot_nt(kc, qrep)
        s_own = jnp.where(jrow <= t, jnp.sum(kn * qf[t:t + 1, :], axis=1, keepdims=True), MASKED)
        m = jnp.maximum(jnp.max(s, axis=0, keepdims=True), jnp.max(s_own, axis=0, keepdims=True))
        p = jnp.exp2(s - m)
        p_own = jnp.exp2(s_own - m)
        l = jnp.sum(p, axis=0, keepdims=True) + jnp.sum(p_own, axis=0, keepdims=True)
        o = jnp.sum(p * vc, axis=0, keepdims=True) + jnp.sum(p_own * vn, axis=0, keepdims=True)
        o_ref[t:t + 1, :] = o / l


def _moba_sample(q, k_new, v_new, cache_k, cache_v, page_table, sel):
    b, n_pages = page_table.shape
    ts = q.shape[0] // b
    _, page, nh, hd = cache_k.shape
    per_blk = MOBA_BLOCK // page
    nslots = ts * MOBA_TOPK * per_blk
    blk = pl.BlockSpec((ts, hd), lambda bb, hh, *_: (bb, hh))
    kern = functools.partial(_moba_sample_kernel, n_pages=n_pages, per_blk=per_blk)
    return pl.pallas_call(
        kern,
        out_shape=jax.ShapeDtypeStruct(q.shape, F32),
        grid_spec=pltpu.PrefetchScalarGridSpec(
            num_scalar_prefetch=2,
            grid=(b, nh),
            in_specs=[blk, blk, blk, pl.BlockSpec(memory_space=pl.ANY), pl.BlockSpec(memory_space=pl.ANY)],
            out_specs=blk,
            scratch_shapes=[pltpu.VMEM((2 * nslots, page, hd), F32), pltpu.VMEM((2 * nslots, page, hd), F32),
                            pltpu.SemaphoreType.DMA((2, 2))]),
        compiler_params=_params(("arbitrary", "arbitrary")),
        name="moba_sample",
    )(sel.reshape(-1), page_table.reshape(-1), q, k_new, v_new, cache_k, cache_v)


def _rope_tables(pos):
    half = HEAD_DIM_A // 2
    inv = ROPE_THETA ** (-jnp.arange(half, dtype=F32) / half)
    ang = pos.astype(F32)[:, None] * inv[None, :]
    cos, sin = jnp.cos(ang), jnp.sin(ang)
    return jnp.concatenate([cos, cos], axis=1), jnp.concatenate([-sin, sin], axis=1)


def _pack_w_in(w_in):
    d = w_in.shape[0]
    a = 3 * N_HEADS_A * HEAD_DIM_A + 2 * N_HEADS_B * DK_B + 2 * N_HEADS_B * DV_B
    ng = 2 * N_HEADS_B
    w_main = (w_in[:, :a].astype(BF16), w_in[:, a + ng:].astype(BF16))
    w_gate = jnp.zeros((d, LANES), BF16).at[:, :ng].set(w_in[:, a:a + ng].astype(BF16))
    return w_main, w_gate


def _layer(x, c_mod, pos, w, *, cache=None, sample=None):
    (g_mix, w_main, w_gate, b_gate, w_out, g_ffn, w_up, w_conv, b_conv, w_down, g_fin) = w
    b, t, d = x.shape
    n = b * t
    wa = N_HEADS_A * HEAD_DIM_A
    dff = w_down.shape[0]
    hb = N_HEADS_B
    sh_m, sc_m, gt_m, sh_f, sc_f, gt_f = jnp.split(c_mod, 6, axis=-1)
    cos2, sin2 = _rope_tables(pos)
    x2d = x.reshape(n, d)

    if sample is None:
        per_seq = lambda a: a.reshape(b, 1, d)
        q, kf, kb16, vf, vb16, qkb, vbb, sga, sgob, gcol, grow, kmean = _inproj(
            x2d, g_mix, per_seq(sh_m), per_seq(sc_m), cos2, sin2, w_main, w_gate, b_gate,
            rows_per_mod=t, rows_per_pos=t)
        kmean = jnp.pad(kmean.reshape(b, t // MOBA_BLOCK, wa), ((0, 0), (0, LANES - t // MOBA_BLOCK), (0, 0)))
        ya, extra = _moba_prompt(q.reshape(b, t, wa), kb16.reshape(b, t, wa), vb16.reshape(b, t, wa), kmean, *cache)
        ya = ya.reshape(n, wa)
        chunk = MLSTM_CHUNK
        nc = t // chunk
        grow3 = grow.reshape(2 * hb, b * nc, chunk).transpose(1, 0, 2)
        c0 = jnp.zeros((b, hb, DV_B, DK_B), F32)
        n0 = jnp.zeros((b, hb, 1, DK_B), F32)
        m0 = jnp.zeros((b, hb, 1, LANES), F32)
        hbo, c_new, n_new, m_new = _mlstm(qkb.reshape(b, t, -1), vbb.reshape(b, t, -1), gcol.reshape(b, t, LANES),
                                          grow3, c0, n0, m0, chunk=chunk, valid_rows=chunk)
        y, tail = _ffn(x2d, ya, hbo.reshape(n, -1), sga, sgob, per_seq(gt_m), per_seq(sh_f), per_seq(sc_f),
                       per_seq(gt_f), g_ffn, g_fin, w_out, w_up, w_conv, b_conv, w_down, batch=b)
        conv_new = tail[:, SUBLANES - (CONV_W - 1):, :]
    else:
        cache_k, cache_v, page_table, kmean, c_st, n_st, m_st, conv0 = sample
        extra = None
        assert n == ROW_TILE and t >= CONV_W - 1
        past = page_table.shape[1] * cache_k.shape[1]
        assert past % MOBA_BLOCK == 0 and t <= MOBA_BLOCK
        per_row = lambda a: jnp.repeat(a, t, axis=0)
        cos2, sin2 = jnp.tile(cos2, (b, 1)), jnp.tile(sin2, (b, 1))
        q, kf, kb16, vf, vb16, qkb, vbb, sga, sgob, gcol, grow, _ = _inproj(
            x2d, g_mix, per_row(sh_m), per_row(sc_m), cos2, sin2, w_main, w_gate, b_gate,
            rows_per_mod=0, rows_per_pos=n)
        qf = q.astype(F32)
        sel = _select_blocks(qf.reshape(b, t, wa), kmean)[..., :MOBA_TOPK]
        ya = _moba_sample(qf, kf, vf, cache_k, cache_v, page_table, sel).astype(BF16)
        chunk = SAMPLE_CHUNK
        pad = lambda a: jnp.pad(a.reshape(b, t, -1), ((0, 0), (0, chunk - t), (0, 0)))
        grow3 = jnp.pad(grow.reshape(2 * hb, b, t).transpose(1, 0, 2), ((0, 0), (0, 0), (0, chunk - t)))
        m0 = jnp.broadcast_to(m_st[:, :, None, None], (b, hb, 1, LANES))
        hbo, c_new, n_new, m_new = _mlstm(pad(qkb), pad(vbb), pad(gcol), grow3, c_st, n_st[:, :, None, :], m0,
                                          chunk=chunk, valid_rows=t)
        hbo = hbo[:, :t].reshape(n, -1)
        zero = jnp.zeros((b, t - 2, dff), F32)
        fix1 = jnp.concatenate([conv0[:, 1:2], zero, zero[:, :1]], axis=1).reshape(n, dff)
        fix2 = jnp.concatenate([conv0, zero], axis=1).reshape(n, dff)
        y, a_full = _ffn(x2d, ya, hbo, sga, sgob, per_row(gt_m), per_row(sh_f), per_row(sc_f), per_row(gt_f),
                         g_ffn, g_fin, w_out, w_up, w_conv, b_conv, w_down, batch=b, seq_rows=t,
                         fix1=fix1, fix2=fix2)
        conv_new = a_full.reshape(b, t, dff)[:, t - (CONV_W - 1):]
    return (y.reshape(b, t, d), kf.reshape(b, t, N_HEADS_A, HEAD_DIM_A), vf.reshape(b, t, N_HEADS_A, HEAD_DIM_A),
            c_new, n_new[:, :, 0, :], m_new[:, :, 0, 0], conv_new, extra)


def kernel(x_prompt, x_sample, cache_k, cache_v, page_table, state_C, state_n, state_m, state_conv,
           c_prompt, c_sample, w_ada, b_ada, g_norm_mix, w_in, b_gates, w_out, g_norm_ffn,
           w_up, w_conv, b_conv, w_down, g_norm_final):
    depth = w_in.shape[0]
    assert depth == 1
    bp, tp, d = x_prompt.shape
    bs, ts, _ = x_sample.shape
    past = page_table.shape[1] * cache_k.shape[2]
    dl = 0
    mod = _ada(jnp.concatenate([c_prompt, c_sample], axis=0), w_ada[dl], b_ada[dl])
    w_main, w_gate = _pack_w_in(w_in[dl])
    b_gate = jnp.zeros((1, LANES), F32).at[0, :2 * N_HEADS_B].set(b_gates[dl])
    w = (g_norm_mix[dl].reshape(1, d), w_main, w_gate, b_gate, w_out[dl].astype(BF16),
         g_norm_ffn[dl].reshape(1, d), w_up[dl].astype(BF16), w_conv[dl], b_conv[dl].reshape(1, -1),
         w_down[dl].astype(BF16), g_norm_final.reshape(1, d))
    yp, kp, vp, cp, np_, mp, cvp, cache_kmean = _layer(
        x_prompt, mod[:bp], jnp.arange(tp, dtype=jnp.int32), w, cache=(cache_k[dl], page_table))
    ys, ks, vs, cs, ns, ms, cvs, _ = _layer(
        x_sample, mod[bp:], past + jnp.arange(ts, dtype=jnp.int32), w,
        sample=(cache_k[dl], cache_v[dl], page_table, cache_kmean,
                state_C[dl], state_n[dl], state_m[dl], state_conv[dl]))
    st = lambda a: a[None]
    return (yp, ys, st(kp), st(vp), st(ks), st(vs), st(cp), st(np_), st(mp), st(cs), st(ns), st(ms),
            st(cvp), st(cvs))
```
